```python
import jax, jax.numpy as jnp
from jax import lax
import numpy as np

D_MODEL = 1024
BATCH = 8
SEQ = 2048
DEPTH = 1

CHUNK = 64
D_CONV = 512
CONV_WIDTH = 31
N_HEADS = 8
HEAD_DIM = 64
D_ATTN = N_HEADS * HEAD_DIM
D_MIX = D_CONV + D_ATTN
Q_BLOCK = 128
D_FF = 2816
EPS = 1e-6
N_IN = 2 * D_CONV + 3 * D_ATTN + N_HEADS

kernel_name = "hybrid_conv_fox_macaron_block"


def rms_norm(x, g):
    xf = x.astype(jnp.float32)
    y = xf * lax.rsqrt(jnp.mean(xf * xf, axis=-1, keepdims=True) + EPS)
    return (y * g.astype(jnp.float32)).astype(x.dtype)


def layer_norm(x, g, b):
    xf = x.astype(jnp.float32)
    mu = jnp.mean(xf, axis=-1, keepdims=True)
    xc = xf - mu
    y = xc * lax.rsqrt(jnp.mean(xc * xc, axis=-1, keepdims=True) + EPS)
    return (y * g.astype(jnp.float32) + b.astype(jnp.float32)).astype(x.dtype)


def swiglu_ffn(h, w13, w2):
    gate, up = jnp.split(h @ w13, 2, axis=-1)
    return (jax.nn.silu(gate) * up) @ w2


def conv_module(a, g, conv_w, conv_b, ln_g, ln_b):
    u = a * jax.nn.sigmoid(g)
    u_pad = jnp.pad(u, ((0, 0), (CONV_WIDTH - 1, 0), (0, 0)))
    y = lax.conv_general_dilated(
        u_pad, conv_w[:, None, :].astype(u.dtype), window_strides=(1,), padding="VALID",
        dimension_numbers=("NWC", "WIO", "NWC"), feature_group_count=D_CONV)
    y = y + conv_b.astype(y.dtype)
    y = layer_norm(y, ln_g, ln_b)
    return jax.nn.silu(y)


def forgetting_attention(q, k, v, f_logit):
    seq = q.shape[1]
    log_f = jax.nn.log_sigmoid(f_logit.astype(jnp.float32))
    cum = jnp.cumsum(log_f, axis=1).transpose(0, 2, 1)
    scale = HEAD_DIM ** -0.5
    outs = []
    for i in range(seq // Q_BLOCK):
        q0, q1 = i * Q_BLOCK, (i + 1) * Q_BLOCK
        qb, kb, vb = q[:, q0:q1], k[:, :q1], v[:, :q1]
        s = jnp.einsum("bqhd,bkhd->bhqk", qb, kb, preferred_element_type=jnp.float32) * scale
        s = s + cum[:, :, q0:q1, None] - cum[:, :, None, :q1]
        qpos = jnp.arange(q0, q1)
        kpos = jnp.arange(q1)
        s = jnp.where(kpos[None, :] <= qpos[:, None], s, -jnp.inf)
        p = jax.nn.softmax(s, axis=-1)
        outs.append(jnp.einsum("bhqk,bkhd->bqhd", p.astype(vb.dtype), vb))
    return jnp.concatenate(outs, axis=1)


def setup_inputs(seed: int = 0) -> dict:
    key = jax.random.key(seed)
    ks = jax.random.split(key, 24)
    f32 = jnp.float32

    def nrm(k, shape, scale):
        return jax.random.normal(k, shape, f32) * scale

    def gain(k, shape):
        return 1.0 + 0.05 * jax.random.normal(k, shape, f32)

    L = DEPTH
    return {
        "x": jax.random.normal(ks[0], (BATCH, SEQ, D_MODEL), f32),
        "ffn1_norm": gain(ks[1], (L, D_MODEL)),
        "ffn1_w13": nrm(ks[2], (L, D_MODEL, 2 * D_FF), D_MODEL ** -0.5),
        "ffn1_w2": nrm(ks[3], (L, D_FF, D_MODEL), D_FF ** -0.5),
        "mix_norm": gain(ks[4], (L, D_MODEL)),
        "w_in": nrm(ks[5], (L, D_MODEL, N_IN), D_MODEL ** -0.5),
        "conv_w": nrm(ks[6], (L, CONV_WIDTH, D_CONV), CONV_WIDTH ** -0.5),
        "conv_b": nrm(ks[7], (L, D_CONV), 0.02),
        "conv_ln_g": gain(ks[8], (L, D_CONV)),
        "conv_ln_b": nrm(ks[9], (L, D_CONV), 0.02),
        "forget_b": jax.random.uniform(ks[10], (L, N_HEADS), f32, minval=1.0, maxval=4.0),
        "out_norm_conv": gain(ks[11], (L, D_CONV)),
        "out_norm_attn": gain(ks[12], (L, D_ATTN)),
        "w_out": nrm(ks[13], (L, D_MIX, D_MODEL), D_MIX ** -0.5),
        "ffn2_norm": gain(ks[14], (L, D_MODEL)),
        "ffn2_w13": nrm(ks[15], (L, D_MODEL, 2 * D_FF), D_MODEL ** -0.5),
        "ffn2_w2": nrm(ks[16], (L, D_FF, D_MODEL), D_FF ** -0.5),
        "final_norm": gain(ks[17], (D_MODEL,)),
    }


def reference(x, ffn1_norm, ffn1_w13, ffn1_w2, mix_norm, w_in, conv_w, conv_b, conv_ln_g,
              conv_ln_b, forget_b, out_norm_conv, out_norm_attn, w_out, ffn2_norm, ffn2_w13,
              ffn2_w2, final_norm):
    bsz, seq, _ = x.shape
    splits = [D_CONV, 2 * D_CONV, 2 * D_CONV + D_ATTN, 2 * D_CONV + 2 * D_ATTN,
              2 * D_CONV + 3 * D_ATTN]
    for l in range(DEPTH):
        x = x + 0.5 * swiglu_ffn(rms_norm(x, ffn1_norm[l]), ffn1_w13[l], ffn1_w2[l])

        h = rms_norm(x, mix_norm[l])
        proj = h @ w_in[l]
        a, g, q, k, v, fl = jnp.split(proj, splits, axis=-1)

        y_conv = conv_module(a, g, conv_w[l], conv_b[l], conv_ln_g[l], conv_ln_b[l])

        heads = (bsz, seq, N_HEADS, HEAD_DIM)
        y_attn = forgetting_attention(q.reshape(heads), k.reshape(heads), v.reshape(heads),
                                      fl + forget_b[l].astype(fl.dtype))
        y_attn = y_attn.reshape(bsz, seq, D_ATTN)

        y = jnp.concatenate([rms_norm(y_conv, out_norm_conv[l]),
                             rms_norm(y_attn, out_norm_attn[l])], axis=-1)
        x = x + y @ w_out[l]

        x = x + 0.5 * swiglu_ffn(rms_norm(x, ffn2_norm[l]), ffn2_w13[l], ffn2_w2[l])
    return rms_norm(x, final_norm)
```

```python
import functools

import numpy as np
import jax
import jax.numpy as jnp
from jax import lax
from jax.experimental import pallas as pl
from jax.experimental.pallas import tpu as pltpu

D_MODEL = 1024
D_CONV = 512
CONV_WIDTH = 31
N_HEADS = 8
HEAD_DIM = 64
D_ATTN = N_HEADS * HEAD_DIM
D_MIX = D_CONV + D_ATTN
D_FF = 2816
EPS = 1e-6

V7X_LANES = 128
V7X_MXU_DIM = 256
V7X_VMEM_BYTES = 64 * 1024 * 1024

PAIR = 2 * HEAD_DIM
N_PAIRS = N_HEADS // 2
F_PAD = V7X_LANES
N_IN_MAIN = 2 * D_CONV + 3 * D_ATTN
HALO = 32

BF16 = jnp.bfloat16
F32 = jnp.float32


def _plan():
    ffn_chunks = []
    c = 0
    while c < D_FF:
        w = min(4 * V7X_MXU_DIM, D_FF - c)
        ffn_chunks.append((c, c + w))
        c += w
    return dict(
        tm=512,
        tc=512,
        tq=256,
        ffn_chunks=tuple(ffn_chunks),
        vmem_limit=V7X_VMEM_BYTES - 8 * 1024 * 1024,
    )


def _rms(x, g):
    return x * lax.rsqrt(jnp.mean(x * x, axis=-1, keepdims=True) + EPS) * g


def _split3(x):
    hi = x.astype(BF16)
    r = x - hi.astype(F32)
    mid = r.astype(BF16)
    lo = (r - mid.astype(F32)).astype(BF16)
    return hi, mid, lo


def _swiglu(h_bf16, w13_ref, w2_ref, chunks):
    acc = None
    for c0, c1 in chunks:
        gate = jnp.dot(h_bf16, w13_ref[:, c0:c1], preferred_element_type=F32)
        up = jnp.dot(h_bf16, w13_ref[:, D_FF + c0:D_FF + c1], preferred_element_type=F32)
        act = (gate * jax.nn.sigmoid(gate) * up).astype(BF16)
        part = jnp.dot(act, w2_ref[c0:c1, :], preferred_element_type=F32)
        acc = part if acc is None else acc + part
    return acc


def _ffn_in_body(x_ref, n1_ref, w13_ref, w2_ref, nm_ref, win_ref, fb_ref, pq_ref, pk_ref,
                 cq_ref, ck_ref,
                 x1_ref, ag_ref, q_ref, k_ref, v_ref, eq_ref, ek_ref,
                 carry_ref, *, chunks, tm):
    x = x_ref[...]
    h = _rms(x, n1_ref[...]).astype(BF16)
    x1 = x + 0.5 * _swiglu(h, w13_ref, w2_ref, chunks)
    x1_ref[...] = x1

    h2 = _rms(x1, nm_ref[...]).astype(BF16)
    proj = jnp.dot(h2, win_ref[...], preferred_element_type=F32)
    ag_ref[...] = proj[:, :2 * D_CONV]
    o = 2 * D_CONV
    q_ref[...] = (proj[:, o:o + D_ATTN] * (HEAD_DIM ** -0.5)).astype(BF16)
    k_ref[...] = proj[:, o + D_ATTN:o + 2 * D_ATTN].astype(BF16)
    v_ref[...] = proj[:, o + 2 * D_ATTN:o + 3 * D_ATTN].astype(BF16)

    z = proj[:, N_IN_MAIN:] + fb_ref[...]
    logf = jnp.minimum(z, 0.0) - jnp.log1p(jnp.exp(-jnp.abs(z)))
    lane = lax.broadcasted_iota(jnp.int32, logf.shape, 1)
    logf = jnp.where(lane < N_HEADS, logf, 0.0)

    row = lax.broadcasted_iota(jnp.int32, (tm, tm), 0)
    col = lax.broadcasted_iota(jnp.int32, (tm, tm), 1)
    tril = jnp.where(row >= col, 1.0, 0.0).astype(BF16)
    d_loc = sum(jnp.dot(tril, part, preferred_element_type=F32) for part in _split3(logf))

    @pl.when(pl.program_id(1) == 0)
    def _():
        carry_ref[...] = jnp.zeros_like(carry_ref)

    d = d_loc + carry_ref[0:1, :]
    carry_ref[0:1, :] = d[tm - 1:tm, :]

    parts = _split3(d)
    eq = sum(jnp.dot(p, pq_ref[i], preferred_element_type=F32) for i, p in enumerate(parts))
    ek = sum(jnp.dot(p, pk_ref[i], preferred_element_type=F32) for i, p in enumerate(parts))
    eq_ref[...] = (eq + cq_ref[...]).astype(BF16)
    ek_ref[...] = (ck_ref[...] - ek).astype(BF16)


def _conv_body(ag_ref, cw_ref, cb_ref, lg_ref, lb_ref, on_ref, y_ref, u_ref, *, tc):
    @pl.when(pl.program_id(1) == 0)
    def _():
        u_ref[0:HALO, :] = jnp.zeros((HALO, D_CONV), F32)

    a = ag_ref[:, :D_CONV]
    g = ag_ref[:, D_CONV:]
    u_ref[HALO:HALO + tc, :] = a * jax.nn.sigmoid(g)

    first = HALO - (CONV_WIDTH - 1)
    acc = jnp.zeros((tc, D_CONV), F32) + cb_ref[...]
    for kk in range(CONV_WIDTH):
        acc = acc + u_ref[first + kk:first + kk + tc, :] * cw_ref[kk:kk + 1, :]
    u_ref[0:HALO, :] = u_ref[tc:tc + HALO, :]

    mu = jnp.mean(acc, axis=-1, keepdims=True)
    xc = acc - mu
    y = xc * lax.rsqrt(jnp.mean(xc * xc, axis=-1, keepdims=True) + EPS) * lg_ref[...] + lb_ref[...]
    y = y * jax.nn.sigmoid(y)
    y_ref[...] = _rms(y, on_ref[...]).astype(BF16)


def _attn_body(q_ref, eq_ref, k_ref, ek_ref, v_ref, o_ref, ka_ref, va_ref, *, seq, tq):
    lane = lax.broadcasted_iota(jnp.int32, (seq, PAIR), 1)
    first_head = lane < HEAD_DIM
    zero = jnp.zeros((seq, PAIR), BF16)
    kk, ee, vv = k_ref[...], ek_ref[...], v_ref[...]
    for hh, keep in enumerate((first_head, jnp.logical_not(first_head))):
        ka_ref[hh, :, 0:PAIR] = jnp.where(keep, kk, zero)
        ka_ref[hh, :, PAIR:2 * PAIR] = jnp.where(keep, ee, zero)
        va_ref[hh] = jnp.where(keep, vv, zero)

    for i in range(seq // tq):
        r0, r1 = i * tq, (i + 1) * tq
        qa = jnp.concatenate([q_ref[r0:r1, :], eq_ref[r0:r1, :]], axis=1)
        row = lax.broadcasted_iota(jnp.int32, (tq, r1), 0) + r0
        col = lax.broadcasted_iota(jnp.int32, (tq, r1), 1)
        causal = col <= row
        out = None
        for hh in range(2):
            s = lax.dot_general(qa, ka_ref[hh, 0:r1, :], (((1,), (1,)), ((), ())),
                                preferred_element_type=F32)
            s = jnp.where(causal, s, -jnp.inf)
            m = jnp.max(s, axis=-1, keepdims=True)
            p = jnp.exp(s - m)
            l = jnp.sum(p, axis=-1, keepdims=True)
            pv = jnp.dot(p.astype(BF16), va_ref[hh, 0:r1, :], preferred_element_type=F32)
            pv = pv * (1.0 / l)
            out = pv if out is None else out + pv
        o_ref[r0:r1, :] = out


def _out_ffn_body(x1_ref, yc_ref, ya_ref, na_ref, wo_ref, n2_ref, w13_ref, w2_ref, nf_ref,
                  o_ref, *, chunks):
    ya = _rms(ya_ref[...], na_ref[...]).astype(BF16)
    y = (jnp.dot(yc_ref[...], wo_ref[0:D_CONV, :], preferred_element_type=F32)
         + jnp.dot(ya, wo_ref[D_CONV:D_MIX, :], preferred_element_type=F32))
    x2 = x1_ref[...] + y
    h = _rms(x2, n2_ref[...]).astype(BF16)
    x3 = x2 + 0.5 * _swiglu(h, w13_ref, w2_ref, chunks)
    o_ref[...] = _rms(x3, nf_ref[...])


def _resident(shape):
    nd = len(shape)
    return pl.BlockSpec(shape, lambda *_: (0,) * nd, pipeline_mode=pl.Buffered(1))


def _placement():
    pq = np.zeros((3, F_PAD, D_ATTN), np.float32)
    pk = np.zeros((3, F_PAD, D_ATTN), np.float32)
    cq = np.zeros((1, D_ATTN), np.float32)
    ck = np.zeros((1, D_ATTN), np.float32)
    for hd in range(N_HEADS):
        base = hd * HEAD_DIM
        for i in range(3):
            pq[i, hd, base + i] = 1.0
            pk[i, hd, base + 3 + i] = 1.0
            cq[0, base + 3 + i] = 1.0
            ck[0, base + i] = 1.0
    return jnp.asarray(pq, BF16), jnp.asarray(pk, BF16), jnp.asarray(cq), jnp.asarray(ck)


def _layer(x, p, final_norm, plan):
    bsz, seq, _ = x.shape
    tm, tc, tq = plan["tm"], plan["tc"], plan["tq"]
    chunks = plan["ffn_chunks"]
    cparams = functools.partial(pltpu.CompilerParams, vmem_limit_bytes=plan["vmem_limit"])
    row2 = lambda v: v.reshape(1, -1)

    w_in = jnp.pad(p["w_in"], ((0, 0), (0, F_PAD - N_HEADS))).astype(BF16)
    fb = jnp.pad(p["forget_b"], (0, F_PAD - N_HEADS)).reshape(1, F_PAD)
    pq, pk, cq, ck = _placement()

    tok = lambda width: pl.BlockSpec((None, tm, width), lambda b, s: (b, s, 0))
    n_in = N_IN_MAIN + F_PAD
    x1, ag, q, k, v, eq, ek = pl.pallas_call(
        functools.partial(_ffn_in_body, chunks=chunks, tm=tm),
        grid=(bsz, seq // tm),
        in_specs=[tok(D_MODEL), _resident((1, D_MODEL)), _resident((D_MODEL, 2 * D_FF)),
                  _resident((D_FF, D_MODEL)), _resident((1, D_MODEL)), _resident((D_MODEL, n_in)),
                  _resident((1, F_PAD)), _resident((3, F_PAD, D_ATTN)),
                  _resident((3, F_PAD, D_ATTN)), _resident((1, D_ATTN)), _resident((1, D_ATTN))],
        out_specs=[tok(D_MODEL), tok(2 * D_CONV)] + [tok(D_ATTN)] * 5,
        out_shape=[jax.ShapeDtypeStruct((bsz, seq, D_MODEL), F32),
                   jax.ShapeDtypeStruct((bsz, seq, 2 * D_CONV), F32)]
                  + [jax.ShapeDtypeStruct((bsz, seq, D_ATTN), BF16)] * 5,
        scratch_shapes=[pltpu.VMEM((8, F_PAD), F32)],
        compiler_params=cparams(dimension_semantics=("arbitrary", "arbitrary")),
        name="ffn_in",
    )(x, row2(p["ffn1_norm"]), p["ffn1_w13"].astype(BF16), p["ffn1_w2"].astype(BF16),
      row2(p["mix_norm"]), w_in, fb, pq, pk, cq, ck)

    ctok = lambda width: pl.BlockSpec((None, tc, width), lambda b, s: (b, s, 0))
    yc = pl.pallas_call(
        functools.partial(_conv_body, tc=tc),
        grid=(bsz, seq // tc),
        in_specs=[ctok(2 * D_CONV), _resident((CONV_WIDTH, D_CONV))] + [_resident((1, D_CONV))] * 4,
        out_specs=ctok(D_CONV),
        out_shape=jax.ShapeDtypeStruct((bsz, seq, D_CONV), BF16),
        scratch_shapes=[pltpu.VMEM((HALO + tc, D_CONV), F32)],
        compiler_params=cparams(dimension_semantics=("arbitrary", "arbitrary")),
        name="conv",
    )(ag, p["conv_w"], row2(p["conv_b"]), row2(p["conv_ln_g"]), row2(p["conv_ln_b"]),
      row2(p["out_norm_conv"]))

    pair = pl.BlockSpec((None, seq, PAIR), lambda b, g: (b, 0, g))
    ya = pl.pallas_call(
        functools.partial(_attn_body, seq=seq, tq=tq),
        grid=(bsz, N_PAIRS),
        in_specs=[pair] * 5,
        out_specs=pair,
        out_shape=jax.ShapeDtypeStruct((bsz, seq, D_ATTN), F32),
        scratch_shapes=[pltpu.VMEM((2, seq, 2 * PAIR), BF16), pltpu.VMEM((2, seq, PAIR), BF16)],
        compiler_params=cparams(dimension_semantics=("arbitrary", "arbitrary")),
        name="attn",
    )(q, eq, k, ek, v)

    out = pl.pallas_call(
        functools.partial(_out_ffn_body, chunks=chunks),
        grid=(bsz, seq // tm),
        in_specs=[tok(D_MODEL), tok(D_CONV), tok(D_ATTN), _resident((1, D_ATTN)),
                  _resident((D_MIX, D_MODEL)), _resident((1, D_MODEL)),
                  _resident((D_MODEL, 2 * D_FF)), _resident((D_FF, D_MODEL)),
                  _resident((1, D_MODEL))],
        out_specs=tok(D_MODEL),
        out_shape=jax.ShapeDtypeStruct((bsz, seq, D_MODEL), F32),
        compiler_params=cparams(dimension_semantics=("arbitrary", "arbitrary")),
        name="out_ffn",
    )(x1, yc, ya, row2(p["out_norm_attn"]), p["w_out"].astype(BF16), row2(p["ffn2_norm"]),
      p["ffn2_w13"].astype(BF16), p["ffn2_w2"].astype(BF16), row2(final_norm))
    return out


def kernel(x, ffn1_norm, ffn1_w13, ffn1_w2, mix_norm, w_in, conv_w, conv_b, conv_ln_g, conv_ln_b,
           forget_b, out_norm_conv, out_norm_attn, w_out, ffn2_norm, ffn2_w13, ffn2_w2, final_norm):
    depth = ffn1_norm.shape[0]
    assert depth == 1, "the fused final RMSNorm assumes a single layer"
    plan = _plan()
    stacked = dict(ffn1_norm=ffn1_norm, ffn1_w13=ffn1_w13, ffn1_w2=ffn1_w2, mix_norm=mix_norm,
                   w_in=w_in, conv_w=conv_w, conv_b=conv_b, conv_ln_g=conv_ln_g,
                   conv_ln_b=conv_ln_b, forget_b=forget_b, out_norm_conv=out_norm_conv,
                   out_norm_attn=out_norm_attn, w_out=w_out, ffn2_norm=ffn2_norm,
                   ffn2_w13=ffn2_w13, ffn2_w2=ffn2_w2)
    layer = {name: arr[0] for name, arr in stacked.items()}
    return _layer(x, layer, final_norm, plan)
```

```python
import functools

import numpy as np
import jax
import jax.numpy as jnp
from jax import lax
from jax.experimental import pallas as pl
from jax.experimental.pallas import tpu as pltpu

D_MODEL = 1024
D_CONV = 512
CONV_WIDTH = 31
N_HEADS = 8
HEAD_DIM = 64
D_ATTN = N_HEADS * HEAD_DIM
D_MIX = D_CONV + D_ATTN
D_FF = 2816
EPS = 1e-6

V7X_LANES = 128
V7X_MXU_DIM = 256
V7X_VMEM_BYTES = 64 * 1024 * 1024

PAIR = 2 * HEAD_DIM
N_PAIRS = N_HEADS // 2
F_PAD = V7X_LANES
N_IN_MAIN = 2 * D_CONV + 3 * D_ATTN
HALO = 32
ONES_ROWS = 16

BF16 = jnp.bfloat16
F32 = jnp.float32


def _plan():
    ffn_chunks = []
    c = 0
    while c < D_FF:
        w = min(4 * V7X_MXU_DIM, D_FF - c)
        ffn_chunks.append((c, c + w))
        c += w
    return dict(
        tm=512,
        tc=512,
        tq=256,
        ffn_chunks=tuple(ffn_chunks),
        vmem_limit=V7X_VMEM_BYTES - 8 * 1024 * 1024,
    )


def _rms(x, g):
    return x * lax.rsqrt(jnp.mean(x * x, axis=-1, keepdims=True) + EPS) * g


def _split3(x):
    hi = x.astype(BF16)
    r = x - hi.astype(F32)
    mid = r.astype(BF16)
    lo = (r - mid.astype(F32)).astype(BF16)
    return hi, mid, lo


def _swiglu(h_bf16, w13_ref, w2_ref, chunks):
    acc = None
    for c0, c1 in chunks:
        gate = jnp.dot(h_bf16, w13_ref[:, c0:c1], preferred_element_type=F32)
        up = jnp.dot(h_bf16, w13_ref[:, D_FF + c0:D_FF + c1], preferred_element_type=F32)
        act = (gate * jax.nn.sigmoid(gate) * up).astype(BF16)
        part = jnp.dot(act, w2_ref[c0:c1, :], preferred_element_type=F32)
        acc = part if acc is None else acc + part
    return acc


def _ffn_in_body(x_ref, n1_ref, w13_ref, w2_ref, nm_ref, win_ref, fb_ref, pq_ref, pk_ref,
                 cq_ref, ck_ref,
                 x1_ref, ag_ref, qt_ref, k_ref, vt_ref, eqt_ref, ek_ref,
                 carry_ref, *, chunks, tm):
    x = x_ref[...]
    h = _rms(x, n1_ref[...]).astype(BF16)
    x1 = x + 0.5 * _swiglu(h, w13_ref, w2_ref, chunks)
    x1_ref[...] = x1

    h2 = _rms(x1, nm_ref[...]).astype(BF16)
    proj = jnp.dot(h2, win_ref[...], preferred_element_type=F32)
    ag_ref[...] = proj[:, :2 * D_CONV]
    o = 2 * D_CONV
    qt_ref[...] = (proj[:, o:o + D_ATTN] * (HEAD_DIM ** -0.5)).T.astype(BF16)
    k_ref[...] = proj[:, o + D_ATTN:o + 2 * D_ATTN].astype(BF16)
    vt_ref[...] = proj[:, o + 2 * D_ATTN:o + 3 * D_ATTN].T.astype(BF16)

    z = proj[:, N_IN_MAIN:] + fb_ref[...]
    logf = jnp.minimum(z, 0.0) - jnp.log1p(jnp.exp(-jnp.abs(z)))
    lane = lax.broadcasted_iota(jnp.int32, logf.shape, 1)
    logf = jnp.where(lane < N_HEADS, logf, 0.0)

    row = lax.broadcasted_iota(jnp.int32, (tm, tm), 0)
    col = lax.broadcasted_iota(jnp.int32, (tm, tm), 1)
    tril = jnp.where(row >= col, 1.0, 0.0).astype(BF16)
    d_loc = sum(jnp.dot(tril, part, preferred_element_type=F32) for part in _split3(logf))

    @pl.when(pl.program_id(1) == 0)
    def _():
        carry_ref[...] = jnp.zeros_like(carry_ref)

    d = d_loc + carry_ref[0:1, :]
    carry_ref[0:1, :] = d[tm - 1:tm, :]

    parts = _split3(d)
    eq = sum(jnp.dot(p, pq_ref[i], preferred_element_type=F32) for i, p in enumerate(parts))
    ek = sum(jnp.dot(p, pk_ref[i], preferred_element_type=F32) for i, p in enumerate(parts))
    eqt_ref[...] = (eq + cq_ref[...]).T.astype(BF16)
    ek_ref[...] = (ck_ref[...] - ek).astype(BF16)


def _conv_body(ag_ref, cw_ref, cb_ref, lg_ref, lb_ref, on_ref, y_ref, u_ref, *, tc):
    @pl.when(pl.program_id(1) == 0)
    def _():
        u_ref[0:HALO, :] = jnp.zeros((HALO, D_CONV), F32)

    a = ag_ref[:, :D_CONV]
    g = ag_ref[:, D_CONV:]
    u_ref[HALO:HALO + tc, :] = a * jax.nn.sigmoid(g)

    first = HALO - (CONV_WIDTH - 1)
    acc = jnp.zeros((tc, D_CONV), F32) + cb_ref[...]
    for kk in range(CONV_WIDTH):
        acc = acc + u_ref[first + kk:first + kk + tc, :] * cw_ref[kk:kk + 1, :]
    u_ref[0:HALO, :] = u_ref[tc:tc + HALO, :]

    mu = jnp.mean(acc, axis=-1, keepdims=True)
    xc = acc - mu
    y = xc * lax.rsqrt(jnp.mean(xc * xc, axis=-1, keepdims=True) + EPS) * lg_ref[...] + lb_ref[...]
    y = y * jax.nn.sigmoid(y)
    y_ref[...] = _rms(y, on_ref[...]).astype(BF16)


def _attn_body(qt_ref, eqt_ref, k_ref, ek_ref, vt_ref, o_ref, ka_ref, va_ref, *, seq, tq):
    ka_ref[:, 0:PAIR] = k_ref[...]
    ka_ref[:, PAIR:2 * PAIR] = ek_ref[...]
    for hh in range(2):
        va_ref[hh, 0:HEAD_DIM, :] = vt_ref[hh * HEAD_DIM:(hh + 1) * HEAD_DIM, :]
        va_ref[hh, HEAD_DIM:HEAD_DIM + ONES_ROWS, :] = jnp.ones((ONES_ROWS, seq), BF16)

    key = lax.broadcasted_iota(jnp.int32, (tq, 2 * tq), 0)
    qry = lax.broadcasted_iota(jnp.int32, (tq, 2 * tq), 1)
    causal = key <= jnp.where(qry < tq, qry, qry - tq)
    zeros = jnp.zeros((HEAD_DIM, tq), BF16)

    def masked_queries(i):
        q, e = qt_ref[:, i * tq:(i + 1) * tq], eqt_ref[:, i * tq:(i + 1) * tq]
        h0 = jnp.concatenate([q[0:HEAD_DIM], zeros, e[0:HEAD_DIM], zeros], axis=0)
        h1 = jnp.concatenate([zeros, q[HEAD_DIM:PAIR], zeros, e[HEAD_DIM:PAIR]], axis=0)
        return jnp.concatenate([h0, h1], axis=1)

    qm = {}

    def scores(i, j):
        if i not in qm:
            qm.clear()
            qm[i] = masked_queries(i)
        s = jnp.dot(ka_ref[j * tq:(j + 1) * tq, :], qm[i], preferred_element_type=F32)
        return jnp.where(causal, s, -jnp.inf) if i == j else s

    tasks = [(i, j) for i in range(seq // tq) for j in range(i + 1)]
    s_next = scores(*tasks[0])
    m = acc = None
    for n, (i, j) in enumerate(tasks):
        s = s_next
        if n + 1 < len(tasks):
            s_next = scores(*tasks[n + 1])
        m_blk = jnp.max(s, axis=0, keepdims=True)
        m_new = m_blk if j == 0 else jnp.maximum(m, m_blk)
        p = jnp.exp(s - m_new).astype(BF16)
        o = [jnp.dot(va_ref[hh, :, j * tq:(j + 1) * tq], p[:, hh * tq:(hh + 1) * tq],
                     preferred_element_type=F32) for hh in range(2)]
        if j == 0:
            acc = o
        else:
            alpha = jnp.exp(m - m_new)
            acc = [acc[hh] * alpha[:, hh * tq:(hh + 1) * tq] + o[hh] for hh in range(2)]
        m = m_new
        if j == i:
            heads = [a[0:HEAD_DIM, :] * (1.0 / a[HEAD_DIM:HEAD_DIM + 1, :]) for a in acc]
            o_ref[i * tq:(i + 1) * tq, :] = jnp.concatenate(heads, axis=0).T


def _out_ffn_body(x1_ref, yc_ref, ya_ref, na_ref, wo_ref, n2_ref, w13_ref, w2_ref, nf_ref,
                  o_ref, *, chunks):
    ya = _rms(ya_ref[...], na_ref[...]).astype(BF16)
    y = (jnp.dot(yc_ref[...], wo_ref[0:D_CONV, :], preferred_element_type=F32)
         + jnp.dot(ya, wo_ref[D_CONV:D_MIX, :], preferred_element_type=F32))
    x2 = x1_ref[...] + y
    h = _rms(x2, n2_ref[...]).astype(BF16)
    x3 = x2 + 0.5 * _swiglu(h, w13_ref, w2_ref, chunks)
    o_ref[...] = _rms(x3, nf_ref[...])


def _resident(shape):
    nd = len(shape)
    return pl.BlockSpec(shape, lambda *_: (0,) * nd, pipeline_mode=pl.Buffered(1))


def _placement():
    pq = np.zeros((3, F_PAD, D_ATTN), np.float32)
    pk = np.zeros((3, F_PAD, D_ATTN), np.float32)
    cq = np.zeros((1, D_ATTN), np.float32)
    ck = np.zeros((1, D_ATTN), np.float32)
    for hd in range(N_HEADS):
        base = hd * HEAD_DIM
        for i in range(3):
            pq[i, hd, base + i] = 1.0
            pk[i, hd, base + 3 + i] = 1.0
            cq[0, base + 3 + i] = 1.0
            ck[0, base + i] = 1.0
    return jnp.asarray(pq, BF16), jnp.asarray(pk, BF16), jnp.asarray(cq), jnp.asarray(ck)


def _layer(x, p, final_norm, plan):
    bsz, seq, _ = x.shape
    tm, tc, tq = plan["tm"], plan["tc"], plan["tq"]
    chunks = plan["ffn_chunks"]
    cparams = functools.partial(pltpu.CompilerParams, vmem_limit_bytes=plan["vmem_limit"])
    row2 = lambda v: v.reshape(1, -1)

    w_in = jnp.pad(p["w_in"], ((0, 0), (0, F_PAD - N_HEADS))).astype(BF16)
    fb = jnp.pad(p["forget_b"], (0, F_PAD - N_HEADS)).reshape(1, F_PAD)
    pq, pk, cq, ck = _placement()

    tok = lambda width: pl.BlockSpec((None, tm, width), lambda b, s: (b, s, 0))
    n_in = N_IN_MAIN + F_PAD
    tok_t = pl.BlockSpec((None, D_ATTN, tm), lambda b, s: (b, 0, s))
    attn_t = jax.ShapeDtypeStruct((bsz, D_ATTN, seq), BF16)
    attn_n = jax.ShapeDtypeStruct((bsz, seq, D_ATTN), BF16)
    x1, ag, qt, k, vt, eqt, ek = pl.pallas_call(
        functools.partial(_ffn_in_body, chunks=chunks, tm=tm),
        grid=(bsz, seq // tm),
        in_specs=[tok(D_MODEL), _resident((1, D_MODEL)), _resident((D_MODEL, 2 * D_FF)),
                  _resident((D_FF, D_MODEL)), _resident((1, D_MODEL)), _resident((D_MODEL, n_in)),
                  _resident((1, F_PAD)), _resident((3, F_PAD, D_ATTN)),
                  _resident((3, F_PAD, D_ATTN)), _resident((1, D_ATTN)), _resident((1, D_ATTN))],
        out_specs=[tok(D_MODEL), tok(2 * D_CONV), tok_t, tok(D_ATTN), tok_t, tok_t, tok(D_ATTN)],
        out_shape=[jax.ShapeDtypeStruct((bsz, seq, D_MODEL), F32),
                   jax.ShapeDtypeStruct((bsz, seq, 2 * D_CONV), F32),
                   attn_t, attn_n, attn_t, attn_t, attn_n],
        scratch_shapes=[pltpu.VMEM((8, F_PAD), F32)],
        compiler_params=cparams(dimension_semantics=("arbitrary", "arbitrary")),
        name="ffn_in",
    )(x, row2(p["ffn1_norm"]), p["ffn1_w13"].astype(BF16), p["ffn1_w2"].astype(BF16),
      row2(p["mix_norm"]), w_in, fb, pq, pk, cq, ck)

    ctok = lambda width: pl.BlockSpec((None, tc, width), lambda b, s: (b, s, 0))
    yc = pl.pallas_call(
        functools.partial(_conv_body, tc=tc),
        grid=(bsz, seq // tc),
        in_specs=[ctok(2 * D_CONV), _resident((CONV_WIDTH, D_CONV))] + [_resident((1, D_CONV))] * 4,
        out_specs=ctok(D_CONV),
        out_shape=jax.ShapeDtypeStruct((bsz, seq, D_CONV), BF16),
        scratch_shapes=[pltpu.VMEM((HALO + tc, D_CONV), F32)],
        compiler_params=cparams(dimension_semantics=("arbitrary", "arbitrary")),
        name="conv",
    )(ag, p["conv_w"], row2(p["conv_b"]), row2(p["conv_ln_g"]), row2(p["conv_ln_b"]),
      row2(p["out_norm_conv"]))

    pair = pl.BlockSpec((None, seq, PAIR), lambda b, g: (b, 0, g))
    pair_t = pl.BlockSpec((None, PAIR, seq), lambda b, g: (b, g, 0))
    ya = pl.pallas_call(
        functools.partial(_attn_body, seq=seq, tq=tq),
        grid=(bsz, N_PAIRS),
        in_specs=[pair_t, pair_t, pair, pair, pair_t],
        out_specs=pair,
        out_shape=jax.ShapeDtypeStruct((bsz, seq, D_ATTN), F32),
        scratch_shapes=[pltpu.VMEM((seq, 2 * PAIR), BF16),
                        pltpu.VMEM((2, HEAD_DIM + ONES_ROWS, seq), BF16)],
        compiler_params=cparams(dimension_semantics=("arbitrary", "arbitrary")),
        name="attn",
    )(qt, eqt, k, ek, vt)

    out = pl.pallas_call(
        functools.partial(_out_ffn_body, chunks=chunks),
        grid=(bsz, seq // tm),
        in_specs=[tok(D_MODEL), tok(D_CONV), tok(D_ATTN), _resident((1, D_ATTN)),
                  _resident((D_MIX, D_MODEL)), _resident((1, D_MODEL)),
                  _resident((D_MODEL, 2 * D_FF)), _resident((D_FF, D_MODEL)),
                  _resident((1, D_MODEL))],
        out_specs=tok(D_MODEL),
        out_shape=jax.ShapeDtypeStruct((bsz, seq, D_MODEL), F32),
        compiler_params=cparams(dimension_semantics=("arbitrary", "arbitrary")),
        name="out_ffn",
    )(x1, yc, ya, row2(p["out_norm_attn"]), p["w_out"].astype(BF16), row2(p["ffn2_norm"]),
      p["ffn2_w13"].astype(BF16), p["ffn2_w2"].astype(BF16), row2(final_norm))
    return out


def kernel(x, ffn1_norm, ffn1_w13, ffn1_w2, mix_norm, w_in, conv_w, conv_b, conv_ln_g, conv_ln_b,
           forget_b, out_norm_conv, out_norm_attn, w_out, ffn2_norm, ffn2_w13, ffn2_w2, final_norm):
    depth = ffn1_norm.shape[0]
    assert depth == 1, "the fused final RMSNorm assumes a single layer"
    plan = _plan()
    stacked = dict(ffn1_norm=ffn1_norm, ffn1_w13=ffn1_w13, ffn1_w2=ffn1_w2, mix_norm=mix_norm,
                   w_in=w_in, conv_w=conv_w, conv_b=conv_b, conv_ln_g=conv_ln_g,
                   conv_ln_b=conv_ln_b, forget_b=forget_b, out_norm_conv=out_norm_conv,
                   out_norm_attn=out_norm_attn, w_out=w_out, ffn2_norm=ffn2_norm,
                   ffn2_w13=ffn2_w13, ffn2_w2=ffn2_w2)
    layer = {name: arr[0] for name, arr in stacked.items()}
    return _layer(x, layer, final_norm, plan)
```

```python
import functools

import numpy as np
import jax
import jax.numpy as jnp
from jax import lax
from jax.experimental import pallas as pl
from jax.experimental.pallas import tpu as pltpu

D_MODEL = 1024
D_CONV = 512
CONV_WIDTH = 31
N_HEADS = 8
HEAD_DIM = 64
D_ATTN = N_HEADS * HEAD_DIM
D_MIX = D_CONV + D_ATTN
D_FF = 2816
EPS = 1e-6

V7X_LANES = 128
V7X_SUBLANES = 8
V7X_MXU_DIM = 256
V7X_VMEM_BYTES = 64 * 1024 * 1024

PAIR = 2 * HEAD_DIM
N_PAIRS = N_HEADS // 2
F_PAD = V7X_LANES
N_IN_MAIN = 2 * D_CONV + 3 * D_ATTN
HALO = 32
CONV_ROWS = 32
ONES_ROWS = 16

BF16 = jnp.bfloat16
F32 = jnp.float32


def _plan():
    ffn_chunks = []
    c = 0
    while c < D_FF:
        w = min(4 * V7X_MXU_DIM, D_FF - c)
        ffn_chunks.append((c, c + w))
        c += w
    return dict(
        tm=512,
        tc=512,
        tq=256,
        ffn_chunks=tuple(ffn_chunks),
        vmem_limit=V7X_VMEM_BYTES - 8 * 1024 * 1024,
    )


def _rms(x, g):
    return x * lax.rsqrt(jnp.mean(x * x, axis=-1, keepdims=True) + EPS) * g


def _split3(x):
    hi = x.astype(BF16)
    r = x - hi.astype(F32)
    mid = r.astype(BF16)
    lo = (r - mid.astype(F32)).astype(BF16)
    return hi, mid, lo


def _swiglu(h_bf16, w13_ref, w2_ref, chunks):
    acc = None
    for c0, c1 in chunks:
        gate = jnp.dot(h_bf16, w13_ref[:, c0:c1], preferred_element_type=F32)
        up = jnp.dot(h_bf16, w13_ref[:, D_FF + c0:D_FF + c1], preferred_element_type=F32)
        act = (gate * jax.nn.sigmoid(gate) * up).astype(BF16)
        part = jnp.dot(act, w2_ref[c0:c1, :], preferred_element_type=F32)
        acc = part if acc is None else acc + part
    return acc


def _ffn_in_body(x_ref, n1_ref, w13_ref, w2_ref, nm_ref, win_ref, fb_ref, pq_ref, pk_ref,
                 cq_ref, ck_ref,
                 x1_ref, ag_ref, qt_ref, k_ref, vt_ref, eqt_ref, ek_ref,
                 carry_ref, *, chunks, tm):
    x = x_ref[...]
    h = _rms(x, n1_ref[...]).astype(BF16)
    x1 = x + 0.5 * _swiglu(h, w13_ref, w2_ref, chunks)
    x1_ref[...] = x1

    h2 = _rms(x1, nm_ref[...]).astype(BF16)
    proj = jnp.dot(h2, win_ref[...], preferred_element_type=F32)
    ag_ref[...] = proj[:, :2 * D_CONV]
    o = 2 * D_CONV
    qt_ref[...] = (proj[:, o:o + D_ATTN] * (HEAD_DIM ** -0.5)).T.astype(BF16)
    k_ref[...] = proj[:, o + D_ATTN:o + 2 * D_ATTN].astype(BF16)
    vt_ref[...] = proj[:, o + 2 * D_ATTN:o + 3 * D_ATTN].T.astype(BF16)

    z = proj[:, N_IN_MAIN:] + fb_ref[...]
    logf = jnp.minimum(z, 0.0) - jnp.log1p(jnp.exp(-jnp.abs(z)))
    lane = lax.broadcasted_iota(jnp.int32, logf.shape, 1)
    logf = jnp.where(lane < N_HEADS, logf, 0.0)

    row = lax.broadcasted_iota(jnp.int32, (tm, tm), 0)
    col = lax.broadcasted_iota(jnp.int32, (tm, tm), 1)
    tril = jnp.where(row >= col, 1.0, 0.0).astype(BF16)
    d_loc = sum(jnp.dot(tril, part, preferred_element_type=F32) for part in _split3(logf))

    @pl.when(pl.program_id(1) == 0)
    def _():
        carry_ref[...] = jnp.zeros_like(carry_ref)

    d = d_loc + carry_ref[0:1, :]
    carry_ref[0:1, :] = d[tm - 1:tm, :]

    parts = _split3(d)
    eq = sum(jnp.dot(p, pq_ref[i], preferred_element_type=F32) for i, p in enumerate(parts))
    ek = sum(jnp.dot(p, pk_ref[i], preferred_element_type=F32) for i, p in enumerate(parts))
    eqt_ref[...] = (eq + cq_ref[...]).T.astype(BF16)
    ek_ref[...] = (ck_ref[...] - ek).astype(BF16)


def _conv_body(ag_ref, cw_ref, cb_ref, lg_ref, lb_ref, on_ref, y_ref, u_ref, s_ref, *, tc):
    @pl.when(pl.program_id(1) == 0)
    def _():
        u_ref[0:HALO, :] = jnp.zeros((HALO, D_CONV), F32)

    u_ref[HALO:HALO + tc, :] = ag_ref[:, :D_CONV] * jax.nn.sigmoid(ag_ref[:, D_CONV:])
    for r in range(1, V7X_SUBLANES):
        s_ref[r - 1] = u_ref[r:r + s_ref.shape[1], :]

    first = HALO - (CONV_WIDTH - 1)
    groups = CONV_ROWS // V7X_SUBLANES
    out = []
    for base in range(0, tc, CONV_ROWS):
        acc = None
        for kk in range(CONV_WIDTH):
            aligned, r = divmod(first + kk, V7X_SUBLANES)
            src = u_ref if r == 0 else s_ref.at[r - 1]
            lo = base + aligned * V7X_SUBLANES
            term = src[lo:lo + CONV_ROWS, :].reshape(groups, V7X_SUBLANES, D_CONV) * cw_ref[kk]
            acc = term if acc is None else acc + term
        out.append(acc.reshape(CONV_ROWS, D_CONV))
    u_ref[0:HALO, :] = u_ref[tc:tc + HALO, :]

    acc = jnp.concatenate(out, axis=0) + cb_ref[...]
    mu = jnp.mean(acc, axis=-1, keepdims=True)
    xc = acc - mu
    y = xc * lax.rsqrt(jnp.mean(xc * xc, axis=-1, keepdims=True) + EPS) * lg_ref[...] + lb_ref[...]
    y = y * jax.nn.sigmoid(y)
    y_ref[...] = _rms(y, on_ref[...]).astype(BF16)


def _attn_body(qt_ref, eqt_ref, k_ref, ek_ref, vt_ref, o_ref, ka_ref, va_ref, *, seq, tq):
    ka_ref[:, 0:PAIR] = k_ref[...]
    ka_ref[:, PAIR:2 * PAIR] = ek_ref[...]
    for hh in range(2):
        va_ref[hh, 0:HEAD_DIM, :] = vt_ref[hh * HEAD_DIM:(hh + 1) * HEAD_DIM, :]
        va_ref[hh, HEAD_DIM:HEAD_DIM + ONES_ROWS, :] = jnp.ones((ONES_ROWS, seq), BF16)

    key = lax.broadcasted_iota(jnp.int32, (tq, 2 * tq), 0)
    qry = lax.broadcasted_iota(jnp.int32, (tq, 2 * tq), 1)
    causal = key <= jnp.where(qry < tq, qry, qry - tq)
    zeros = jnp.zeros((HEAD_DIM, tq), BF16)

    def masked_queries(i):
        q, e = qt_ref[:, i * tq:(i + 1) * tq], eqt_ref[:, i * tq:(i + 1) * tq]
        h0 = jnp.concatenate([q[0:HEAD_DIM], zeros, e[0:HEAD_DIM], zeros], axis=0)
        h1 = jnp.concatenate([zeros, q[HEAD_DIM:PAIR], zeros, e[HEAD_DIM:PAIR]], axis=0)
        return jnp.concatenate([h0, h1], axis=1)

    qm = {}

    def scores(i, j):
        if i not in qm:
            qm.clear()
            qm[i] = masked_queries(i)
        s = jnp.dot(ka_ref[j * tq:(j + 1) * tq, :], qm[i], preferred_element_type=F32)
        return jnp.where(causal, s, -jnp.inf) if i == j else s

    tasks = [(i, j) for i in range(seq // tq) for j in range(i + 1)]
    s_next = scores(*tasks[0])
    m = acc = None
    for n, (i, j) in enumerate(tasks):
        s = s_next
        if n + 1 < len(tasks):
            s_next = scores(*tasks[n + 1])
        m_blk = jnp.max(s, axis=0, keepdims=True)
        m_new = m_blk if j == 0 else jnp.maximum(m, m_blk)
        p = jnp.exp(s - m_new).astype(BF16)
        o = [jnp.dot(va_ref[hh, :, j * tq:(j + 1) * tq], p[:, hh * tq:(hh + 1) * tq],
                     preferred_element_type=F32) for hh in range(2)]
        if j == 0:
            acc = o
        else:
            alpha = jnp.exp(m - m_new)
            acc = [acc[hh] * alpha[:, hh * tq:(hh + 1) * tq] + o[hh] for hh in range(2)]
        m = m_new
        if j == i:
            heads = [a[0:HEAD_DIM, :] * (1.0 / a[HEAD_DIM:HEAD_DIM + 1, :]) for a in acc]
            o_ref[i * tq:(i + 1) * tq, :] = jnp.concatenate(heads, axis=0).T


def _out_ffn_body(x1_ref, yc_ref, ya_ref, na_ref, wo_ref, n2_ref, w13_ref, w2_ref, nf_ref,
                  o_ref, *, chunks):
    ya = _rms(ya_ref[...], na_ref[...]).astype(BF16)
    y = (jnp.dot(yc_ref[...], wo_ref[0:D_CONV, :], preferred_element_type=F32)
         + jnp.dot(ya, wo_ref[D_CONV:D_MIX, :], preferred_element_type=F32))
    x2 = x1_ref[...] + y
    h = _rms(x2, n2_ref[...]).astype(BF16)
    x3 = x2 + 0.5 * _swiglu(h, w13_ref, w2_ref, chunks)
    o_ref[...] = _rms(x3, nf_ref[...])


def _resident(shape):
    nd = len(shape)
    return pl.BlockSpec(shape, lambda *_: (0,) * nd, pipeline_mode=pl.Buffered(1))


def _placement():
    pq = np.zeros((3, F_PAD, D_ATTN), np.float32)
    pk = np.zeros((3, F_PAD, D_ATTN), np.float32)
    cq = np.zeros((1, D_ATTN), np.float32)
    ck = np.zeros((1, D_ATTN), np.float32)
    for hd in range(N_HEADS):
        base = hd * HEAD_DIM
        for i in range(3):
            pq[i, hd, base + i] = 1.0
            pk[i, hd, base + 3 + i] = 1.0
            cq[0, base + 3 + i] = 1.0
            ck[0, base + i] = 1.0
    return jnp.asarray(pq, BF16), jnp.asarray(pk, BF16), jnp.asarray(cq), jnp.asarray(ck)


def _layer(x, p, final_norm, plan):
    bsz, seq, _ = x.shape
    tm, tc, tq = plan["tm"], plan["tc"], plan["tq"]
    chunks = plan["ffn_chunks"]
    cparams = functools.partial(pltpu.CompilerParams, vmem_limit_bytes=plan["vmem_limit"])
    row2 = lambda v: v.reshape(1, -1)

    w_in = jnp.pad(p["w_in"], ((0, 0), (0, F_PAD - N_HEADS))).astype(BF16)
    fb = jnp.pad(p["forget_b"], (0, F_PAD - N_HEADS)).reshape(1, F_PAD)
    pq, pk, cq, ck = _placement()

    tok = lambda width: pl.BlockSpec((None, tm, width), lambda b, s: (b, s, 0))
    n_in = N_IN_MAIN + F_PAD
    tok_t = pl.BlockSpec((None, D_ATTN, tm), lambda b, s: (b, 0, s))
    attn_t = jax.ShapeDtypeStruct((bsz, D_ATTN, seq), BF16)
    attn_n = jax.ShapeDtypeStruct((bsz, seq, D_ATTN), BF16)
    x1, ag, qt, k, vt, eqt, ek = pl.pallas_call(
        functools.partial(_ffn_in_body, chunks=chunks, tm=tm),
        grid=(bsz, seq // tm),
        in_specs=[tok(D_MODEL), _resident((1, D_MODEL)), _resident((D_MODEL, 2 * D_FF)),
                  _resident((D_FF, D_MODEL)), _resident((1, D_MODEL)), _resident((D_MODEL, n_in)),
                  _resident((1, F_PAD)), _resident((3, F_PAD, D_ATTN)),
                  _resident((3, F_PAD, D_ATTN)), _resident((1, D_ATTN)), _resident((1, D_ATTN))],
        out_specs=[tok(D_MODEL), tok(2 * D_CONV), tok_t, tok(D_ATTN), tok_t, tok_t, tok(D_ATTN)],
        out_shape=[jax.ShapeDtypeStruct((bsz, seq, D_MODEL), F32),
                   jax.ShapeDtypeStruct((bsz, seq, 2 * D_CONV), F32),
                   attn_t, attn_n, attn_t, attn_t, attn_n],
        scratch_shapes=[pltpu.VMEM((V7X_SUBLANES, F_PAD), F32)],
        compiler_params=cparams(dimension_semantics=("arbitrary", "arbitrary")),
        name="ffn_in",
    )(x, row2(p["ffn1_norm"]), p["ffn1_w13"].astype(BF16), p["ffn1_w2"].astype(BF16),
      row2(p["mix_norm"]), w_in, fb, pq, pk, cq, ck)

    ctok = lambda width: pl.BlockSpec((None, tc, width), lambda b, s: (b, s, 0))
    yc = pl.pallas_call(
        functools.partial(_conv_body, tc=tc),
        grid=(bsz, seq // tc),
        in_specs=[ctok(2 * D_CONV), _resident((CONV_WIDTH, V7X_SUBLANES, D_CONV))]
                 + [_resident((1, D_CONV))] * 4,
        out_specs=ctok(D_CONV),
        out_shape=jax.ShapeDtypeStruct((bsz, seq, D_CONV), BF16),
        scratch_shapes=[pltpu.VMEM((HALO + tc, D_CONV), F32),
                        pltpu.VMEM((V7X_SUBLANES - 1, HALO + tc - V7X_SUBLANES, D_CONV), F32)],
        compiler_params=cparams(dimension_semantics=("arbitrary", "arbitrary")),
        name="conv",
    )(ag, jnp.broadcast_to(p["conv_w"][:, None, :], (CONV_WIDTH, V7X_SUBLANES, D_CONV)),
      row2(p["conv_b"]), row2(p["conv_ln_g"]), row2(p["conv_ln_b"]), row2(p["out_norm_conv"]))

    pair = pl.BlockSpec((None, seq, PAIR), lambda b, g: (b, 0, g))
    pair_t = pl.BlockSpec((None, PAIR, seq), lambda b, g: (b, g, 0))
    ya = pl.pallas_call(
        functools.partial(_attn_body, seq=seq, tq=tq),
        grid=(bsz, N_PAIRS),
        in_specs=[pair_t, pair_t, pair, pair, pair_t],
        out_specs=pair,
        out_shape=jax.ShapeDtypeStruct((bsz, seq, D_ATTN), F32),
        scratch_shapes=[pltpu.VMEM((seq, 2 * PAIR), BF16),
                        pltpu.VMEM((2, HEAD_DIM + ONES_ROWS, seq), BF16)],
        compiler_params=cparams(dimension_semantics=("arbitrary", "arbitrary")),
        name="attn",
    )(qt, eqt, k, ek, vt)

    out = pl.pallas_call(
        functools.partial(_out_ffn_body, chunks=chunks),
        grid=(bsz, seq // tm),
        in_specs=[tok(D_MODEL), tok(D_CONV), tok(D_ATTN), _resident((1, D_ATTN)),
                  _resident((D_MIX, D_MODEL)), _resident((1, D_MODEL)),
                  _resident((D_MODEL, 2 * D_FF)), _resident((D_FF, D_MODEL)),
                  _resident((1, D_MODEL))],
        out_specs=tok(D_MODEL),
        out_shape=jax.ShapeDtypeStruct((bsz, seq, D_MODEL), F32),
        compiler_params=cparams(dimension_semantics=("arbitrary", "arbitrary")),
        name="out_ffn",
    )(x1, yc, ya, row2(p["out_norm_attn"]), p["w_out"].astype(BF16), row2(p["ffn2_norm"]),
      p["ffn2_w13"].astype(BF16), p["ffn2_w2"].astype(BF16), row2(final_norm))
    return out


def kernel(x, ffn1_norm, ffn1_w13, ffn1_w2, mix_norm, w_in, conv_w, conv_b, conv_ln_g, conv_ln_b,
           forget_b, out_norm_conv, out_norm_attn, w_out, ffn2_norm, ffn2_w13, ffn2_w2, final_norm):
    depth = ffn1_norm.shape[0]
    assert depth == 1, "the fused final RMSNorm assumes a single layer"
    plan = _plan()
    stacked = dict(ffn1_norm=ffn1_norm, ffn1_w13=ffn1_w13, ffn1_w2=ffn1_w2, mix_norm=mix_norm,
                   w_in=w_in, conv_w=conv_w, conv_b=conv_b, conv_ln_g=conv_ln_g,
                   conv_ln_b=conv_ln_b, forget_b=forget_b, out_norm_conv=out_norm_conv,
                   out_norm_attn=out_norm_attn, w_out=w_out, ffn2_norm=ffn2_norm,
                   ffn2_w13=ffn2_w13, ffn2_w2=ffn2_w2)
    layer = {name: arr[0] for name, arr in stacked.items()}
    return _layer(x, layer, final_norm, plan)
```

```python
import functools

import numpy as np
import jax
import jax.numpy as jnp
from jax import lax
from jax.experimental import pallas as pl
from jax.experimental.pallas import tpu as pltpu

D_MODEL = 1024
D_CONV = 512
CONV_WIDTH = 31
N_HEADS = 8
HEAD_DIM = 64
D_ATTN = N_HEADS * HEAD_DIM
D_MIX = D_CONV + D_ATTN
D_FF = 2816
EPS = 1e-6

V7X_LANES = 128
V7X_SUBLANES = 8
V7X_MXU_DIM = 256
V7X_VMEM_BYTES = 64 * 1024 * 1024

PAIR = 2 * HEAD_DIM
N_PAIRS = N_HEADS // 2
F_PAD = V7X_LANES
N_IN_MAIN = 2 * D_CONV + 3 * D_ATTN
HALO = 32
CONV_ROWS = 32
ONES_ROWS = 16

BF16 = jnp.bfloat16
F32 = jnp.float32


def _plan():
    def hidden_chunks(mxu_tiles):
        step = mxu_tiles * V7X_MXU_DIM
        return tuple((c, min(c + step, D_FF)) for c in range(0, D_FF, step))

    return dict(
        tm=512,
        tq=256,
        ffn_chunks=hidden_chunks(4),
        ffn_in_chunks=hidden_chunks(1),
        vmem_limit=V7X_VMEM_BYTES - 8 * 1024 * 1024,
    )


def _rms(x, g):
    return x * lax.rsqrt(jnp.mean(x * x, axis=-1, keepdims=True) + EPS) * g


def _split3(x):
    hi = x.astype(BF16)
    r = x - hi.astype(F32)
    mid = r.astype(BF16)
    lo = (r - mid.astype(F32)).astype(BF16)
    return hi, mid, lo


def _swiglu(h_bf16, w13_ref, w2_ref, chunks, side_work=None):
    acc = None
    zero = None
    for c, (c0, c1) in enumerate(chunks):
        gate = jnp.dot(h_bf16, w13_ref[:, c0:c1], preferred_element_type=F32)
        up = jnp.dot(h_bf16, w13_ref[:, D_FF + c0:D_FF + c1], preferred_element_type=F32)
        if zero is not None:
            gate = gate + zero
        zero = side_work[c]() if side_work is not None else None
        act = (gate * jax.nn.sigmoid(gate) * up).astype(BF16)
        part = jnp.dot(act, w2_ref[c0:c1, :], preferred_element_type=F32)
        acc = part if acc is None else acc + part
    return acc


def _zero_after(v):
    bits = pltpu.bitcast(v[0:V7X_SUBLANES, 0:V7X_LANES], jnp.uint32)
    bits = lax.shift_right_logical(lax.shift_right_logical(bits, jnp.uint32(16)), jnp.uint32(16))
    return pltpu.bitcast(bits, F32)[0:1, 0:1]


def _conv_rows(u_ref, s_ref, cw_ref, base):
    first = HALO - (CONV_WIDTH - 1)
    groups = CONV_ROWS // V7X_SUBLANES
    acc = None
    for kk in range(CONV_WIDTH):
        aligned, r = divmod(first + kk, V7X_SUBLANES)
        src = u_ref if r == 0 else s_ref.at[r - 1]
        lo = base + aligned * V7X_SUBLANES
        term = src[lo:lo + CONV_ROWS, :].reshape(groups, V7X_SUBLANES, D_CONV) * cw_ref[kk]
        acc = term if acc is None else acc + term
    return acc.reshape(CONV_ROWS, D_CONV)


def _ffn_in_body(x_ref, n1_ref, w13_ref, w2_ref, nm_ref, win_ref, fb_ref, pq_ref, pk_ref,
                 cq_ref, ck_ref, cw_ref, cb_ref, lg_ref, lb_ref, on_ref,
                 x1_ref, yc_ref, qt_ref, k_ref, vt_ref, eqt_ref, ek_ref,
                 carry_ref, u_ref, s_ref, cacc_ref, *, chunks, tm, tiles_per_seq):
    n = pl.program_id(0)

    @pl.when(n == 0)
    def _():
        u_ref[...] = jnp.zeros_like(u_ref)
        carry_ref[...] = jnp.zeros_like(carry_ref)

    def tile_sum(v):
        v = v.reshape(-1, V7X_SUBLANES, v.shape[-1]).sum(axis=0)
        return sum(v[:, g:g + V7X_LANES] for g in range(0, v.shape[-1], V7X_LANES))

    def shift_copies():
        for r in range(1, V7X_SUBLANES):
            s_ref[r - 1] = u_ref[r:r + s_ref.shape[1], :]
        return None

    def conv_group(bases):
        def run():
            seen = None
            for base in bases:
                rows = _conv_rows(u_ref, s_ref, cw_ref, base)
                cacc_ref[base:base + CONV_ROWS, :] = rows
                seen = tile_sum(rows) if seen is None else seen + tile_sum(rows)
            return _zero_after(seen)
        return run

    def conv_epilogue():
        cv = cacc_ref[...] + cb_ref[...]
        mu = jnp.mean(cv, axis=-1, keepdims=True)
        xc = cv - mu
        y = xc * lax.rsqrt(jnp.mean(xc * xc, axis=-1, keepdims=True) + EPS) * lg_ref[...] + lb_ref[...]
        y = y * jax.nn.sigmoid(y)
        y = _rms(y, on_ref[...])
        yc_ref[...] = y.astype(BF16)
        return _zero_after(tile_sum(y))

    bases = list(range(0, tm, CONV_ROWS))
    n_groups = len(chunks) - 3
    per_group = -(-len(bases) // n_groups)
    side_work = ([shift_copies]
                 + [conv_group(bases[g * per_group:(g + 1) * per_group]) for g in range(n_groups)]
                 + [conv_epilogue, lambda: None])

    x = x_ref[...]
    h = _rms(x, n1_ref[...]).astype(BF16)
    x1 = x + 0.5 * _swiglu(h, w13_ref, w2_ref, chunks, side_work)
    x1_ref[...] = x1

    h2 = _rms(x1, nm_ref[...]).astype(BF16)
    proj = jnp.dot(h2, win_ref[...], preferred_element_type=F32)

    opens_sequence = lax.rem(n, tiles_per_seq) == 0
    u_ref[0:HALO, :] = jnp.where(opens_sequence, 0.0, u_ref[tm:tm + HALO, :])
    u_ref[HALO:HALO + tm, :] = proj[:, :D_CONV] * jax.nn.sigmoid(proj[:, D_CONV:2 * D_CONV])
    o = 2 * D_CONV
    qt_ref[...] = (proj[:, o:o + D_ATTN] * (HEAD_DIM ** -0.5)).T.astype(BF16)
    k_ref[...] = proj[:, o + D_ATTN:o + 2 * D_ATTN].astype(BF16)
    vt_ref[...] = proj[:, o + 2 * D_ATTN:o + 3 * D_ATTN].T.astype(BF16)

    z = proj[:, N_IN_MAIN:] + fb_ref[...]
    logf = jnp.minimum(z, 0.0) - jnp.log1p(jnp.exp(-jnp.abs(z)))
    lane = lax.broadcasted_iota(jnp.int32, logf.shape, 1)
    logf = jnp.where(lane < N_HEADS, logf, 0.0)

    row = lax.broadcasted_iota(jnp.int32, (tm, tm), 0)
    col = lax.broadcasted_iota(jnp.int32, (tm, tm), 1)
    tril = jnp.where(row >= col, 1.0, 0.0).astype(BF16)
    d_loc = sum(jnp.dot(tril, part, preferred_element_type=F32) for part in _split3(logf))

    d = d_loc + jnp.where(opens_sequence, 0.0, carry_ref[0:1, :])
    carry_ref[0:1, :] = d[tm - 1:tm, :]

    parts = _split3(d)
    eq = sum(jnp.dot(p, pq_ref[i], preferred_element_type=F32) for i, p in enumerate(parts))
    ek = sum(jnp.dot(p, pk_ref[i], preferred_element_type=F32) for i, p in enumerate(parts))
    eqt = (eq + cq_ref[...]).T.astype(BF16)
    ekn = (ck_ref[...] - ek).astype(BF16)

    @pl.when(n < pl.num_programs(0) - 1)
    def _():
        eqt_ref[...] = eqt
        ek_ref[...] = ekn


def _attn_body(qt_ref, eqt_ref, k_ref, ek_ref, vt_ref, o_ref, ka_ref, va_ref, *, seq, tq):
    ka_ref[:, 0:PAIR] = k_ref[...]
    ka_ref[:, PAIR:2 * PAIR] = ek_ref[...]
    for hh in range(2):
        va_ref[hh, 0:HEAD_DIM, :] = vt_ref[hh * HEAD_DIM:(hh + 1) * HEAD_DIM, :]
        va_ref[hh, HEAD_DIM:HEAD_DIM + ONES_ROWS, :] = jnp.ones((ONES_ROWS, seq), BF16)

    key = lax.broadcasted_iota(jnp.int32, (tq, 2 * tq), 0)
    qry = lax.broadcasted_iota(jnp.int32, (tq, 2 * tq), 1)
    causal = key <= jnp.where(qry < tq, qry, qry - tq)
    zeros = jnp.zeros((HEAD_DIM, tq), BF16)

    def masked_queries(i):
        q, e = qt_ref[:, i * tq:(i + 1) * tq], eqt_ref[:, i * tq:(i + 1) * tq]
        h0 = jnp.concatenate([q[0:HEAD_DIM], zeros, e[0:HEAD_DIM], zeros], axis=0)
        h1 = jnp.concatenate([zeros, q[HEAD_DIM:PAIR], zeros, e[HEAD_DIM:PAIR]], axis=0)
        return jnp.concatenate([h0, h1], axis=1)

    qm = {}

    def scores(i, j):
        if i not in qm:
            qm.clear()
            qm[i] = masked_queries(i)
        s = jnp.dot(ka_ref[j * tq:(j + 1) * tq, :], qm[i], preferred_element_type=F32)
        return jnp.where(causal, s, -jnp.inf) if i == j else s

    tasks = [(i, j) for i in range(seq // tq) for j in range(i + 1)]
    s_next = scores(*tasks[0])
    m = acc = None
    for n, (i, j) in enumerate(tasks):
        s = s_next
        if n + 1 < len(tasks):
            s_next = scores(*tasks[n + 1])
        m_blk = jnp.max(s, axis=0, keepdims=True)
        m_new = m_blk if j == 0 else jnp.maximum(m, m_blk)
        p = jnp.exp(s - m_new).astype(BF16)
        o = [jnp.dot(va_ref[hh, :, j * tq:(j + 1) * tq], p[:, hh * tq:(hh + 1) * tq],
                     preferred_element_type=F32) for hh in range(2)]
        if j == 0:
            acc = o
        else:
            alpha = jnp.exp(m - m_new)
            acc = [acc[hh] * alpha[:, hh * tq:(hh + 1) * tq] + o[hh] for hh in range(2)]
        m = m_new
        if j == i:
            heads = [a[0:HEAD_DIM, :] * (1.0 / a[HEAD_DIM:HEAD_DIM + 1, :]) for a in acc]
            o_ref[i * tq:(i + 1) * tq, :] = jnp.concatenate(heads, axis=0).T


def _out_ffn_body(x1_ref, yc_ref, ya_ref, na_ref, wo_ref, n2_ref, w13_ref, w2_ref, nf_ref,
                  o_ref, *, chunks):
    ya = _rms(ya_ref[...], na_ref[...]).astype(BF16)
    y = (jnp.dot(yc_ref[...], wo_ref[0:D_CONV, :], preferred_element_type=F32)
         + jnp.dot(ya, wo_ref[D_CONV:D_MIX, :], preferred_element_type=F32))
    x2 = x1_ref[...] + y
    h = _rms(x2, n2_ref[...]).astype(BF16)
    x3 = x2 + 0.5 * _swiglu(h, w13_ref, w2_ref, chunks)
    o_ref[...] = _rms(x3, nf_ref[...])


def _resident(shape):
    nd = len(shape)
    return pl.BlockSpec(shape, lambda *_: (0,) * nd, pipeline_mode=pl.Buffered(1))


def _placement():
    pq = np.zeros((3, F_PAD, D_ATTN), np.float32)
    pk = np.zeros((3, F_PAD, D_ATTN), np.float32)
    cq = np.zeros((1, D_ATTN), np.float32)
    ck = np.zeros((1, D_ATTN), np.float32)
    for hd in range(N_HEADS):
        base = hd * HEAD_DIM
        for i in range(3):
            pq[i, hd, base + i] = 1.0
            pk[i, hd, base + 3 + i] = 1.0
            cq[0, base + 3 + i] = 1.0
            ck[0, base + i] = 1.0
    return jnp.asarray(pq, BF16), jnp.asarray(pk, BF16), jnp.asarray(cq), jnp.asarray(ck)


def _layer(x, p, final_norm, plan):
    bsz, seq, _ = x.shape
    tm, tq = plan["tm"], plan["tq"]
    chunks = plan["ffn_chunks"]
    cparams = functools.partial(pltpu.CompilerParams, vmem_limit_bytes=plan["vmem_limit"])
    row2 = lambda v: v.reshape(1, -1)

    w_in = jnp.pad(p["w_in"], ((0, 0), (0, F_PAD - N_HEADS))).astype(BF16)
    fb = jnp.pad(p["forget_b"], (0, F_PAD - N_HEADS)).reshape(1, F_PAD)
    pq, pk, cq, ck = _placement()

    tok = lambda width: pl.BlockSpec((None, tm, width), lambda b, s: (b, s, 0))
    n_in = N_IN_MAIN + F_PAD
    tiles_per_seq = seq // tm
    n_tiles = bsz * tiles_per_seq
    cur = lambda n: jnp.minimum(n, n_tiles - 1)
    flat = lambda width: pl.BlockSpec((tm, width), lambda n: (cur(n), 0))
    flat_t = pl.BlockSpec((None, D_ATTN, tm),
                          lambda n: (cur(n) // tiles_per_seq, 0, cur(n) % tiles_per_seq))
    lagged = pl.BlockSpec((tm, D_CONV), lambda n: (jnp.maximum(n - 1, 0), 0))
    attn_t = jax.ShapeDtypeStruct((bsz, D_ATTN, seq), BF16)
    attn_n = jax.ShapeDtypeStruct((bsz * seq, D_ATTN), BF16)
    x1, yc, qt, k, vt, eqt, ek = pl.pallas_call(
        functools.partial(_ffn_in_body, chunks=plan["ffn_in_chunks"], tm=tm,
                          tiles_per_seq=tiles_per_seq),
        grid=(n_tiles + 1,),
        in_specs=[flat(D_MODEL), _resident((1, D_MODEL)), _resident((D_MODEL, 2 * D_FF)),
                  _resident((D_FF, D_MODEL)), _resident((1, D_MODEL)), _resident((D_MODEL, n_in)),
                  _resident((1, F_PAD)), _resident((3, F_PAD, D_ATTN)),
                  _resident((3, F_PAD, D_ATTN)), _resident((1, D_ATTN)), _resident((1, D_ATTN)),
                  _resident((CONV_WIDTH, V7X_SUBLANES, D_CONV))] + [_resident((1, D_CONV))] * 4,
        out_specs=[flat(D_MODEL), lagged, flat_t, flat(D_ATTN), flat_t, flat_t, flat(D_ATTN)],
        out_shape=[jax.ShapeDtypeStruct((bsz * seq, D_MODEL), F32),
                   jax.ShapeDtypeStruct((bsz * seq, D_CONV), BF16),
                   attn_t, attn_n, attn_t, attn_t, attn_n],
        scratch_shapes=[pltpu.VMEM((V7X_SUBLANES, F_PAD), F32),
                        pltpu.VMEM((HALO + tm, D_CONV), F32),
                        pltpu.VMEM((V7X_SUBLANES - 1, HALO + tm - V7X_SUBLANES, D_CONV), F32),
                        pltpu.VMEM((tm, D_CONV), F32)],
        compiler_params=cparams(dimension_semantics=("arbitrary",)),
        name="ffn_in",
    )(x.reshape(bsz * seq, D_MODEL), row2(p["ffn1_norm"]), p["ffn1_w13"].astype(BF16),
      p["ffn1_w2"].astype(BF16), row2(p["mix_norm"]), w_in, fb, pq, pk, cq, ck,
      jnp.broadcast_to(p["conv_w"][:, None, :], (CONV_WIDTH, V7X_SUBLANES, D_CONV)),
      row2(p["conv_b"]), row2(p["conv_ln_g"]), row2(p["conv_ln_b"]), row2(p["out_norm_conv"]))
    x1 = x1.reshape(bsz, seq, D_MODEL)
    yc = yc.reshape(bsz, seq, D_CONV)
    k = k.reshape(bsz, seq, D_ATTN)
    ek = ek.reshape(bsz, seq, D_ATTN)

    pair = pl.BlockSpec((None, seq, PAIR), lambda b, g: (b, 0, g))
    pair_t = pl.BlockSpec((None, PAIR, seq), lambda b, g: (b, g, 0))
    ya = pl.pallas_call(
        functools.partial(_attn_body, seq=seq, tq=tq),
        grid=(bsz, N_PAIRS),
        in_specs=[pair_t, pair_t, pair, pair, pair_t],
        out_specs=pair,
        out_shape=jax.ShapeDtypeStruct((bsz, seq, D_ATTN), F32),
        scratch_shapes=[pltpu.VMEM((seq, 2 * PAIR), BF16),
                        pltpu.VMEM((2, HEAD_DIM + ONES_ROWS, seq), BF16)],
        compiler_params=cparams(dimension_semantics=("arbitrary", "arbitrary")),
        name="attn",
    )(qt, eqt, k, ek, vt)

    out = pl.pallas_call(
        functools.partial(_out_ffn_body, chunks=chunks),
        grid=(bsz, seq // tm),
        in_specs=[tok(D_MODEL), tok(D_CONV), tok(D_ATTN), _resident((1, D_ATTN)),
                  _resident((D_MIX, D_MODEL)), _resident((1, D_MODEL)),
                  _resident((D_MODEL, 2 * D_FF)), _resident((D_FF, D_MODEL)),
                  _resident((1, D_MODEL))],
        out_specs=tok(D_MODEL),
        out_shape=jax.ShapeDtypeStruct((bsz, seq, D_MODEL), F32),
        compiler_params=cparams(dimension_semantics=("arbitrary", "arbitrary")),
        name="out_ffn",
    )(x1, yc, ya, row2(p["out_norm_attn"]), p["w_out"].astype(BF16), row2(p["ffn2_norm"]),
      p["ffn2_w13"].astype(BF16), p["ffn2_w2"].astype(BF16), row2(final_norm))
    return out


def kernel(x, ffn1_norm, ffn1_w13, ffn1_w2, mix_norm, w_in, conv_w, conv_b, conv_ln_g, conv_ln_b,
           forget_b, out_norm_conv, out_norm_attn, w_out, ffn2_norm, ffn2_w13, ffn2_w2, final_norm):
    depth = ffn1_norm.shape[0]
    assert depth == 1, "the fused final RMSNorm assumes a single layer"
    plan = _plan()
    stacked = dict(ffn1_norm=ffn1_norm, ffn1_w13=ffn1_w13, ffn1_w2=ffn1_w2, mix_norm=mix_norm,
                   w_in=w_in, conv_w=conv_w, conv_b=conv_b, conv_ln_g=conv_ln_g,
                   conv_ln_b=conv_ln_b, forget_b=forget_b, out_norm_conv=out_norm_conv,
                   out_norm_attn=out_norm_attn, w_out=w_out, ffn2_norm=ffn2_norm,
                   ffn2_w13=ffn2_w13, ffn2_w2=ffn2_w2)
    layer = {name: arr[0] for name, arr in stacked.items()}
    return _layer(x, layer, final_norm, plan)
```

```python
import functools

import numpy as np
import jax
import jax.numpy as jnp
from jax import lax
from jax.experimental import pallas as pl
from jax.experimental.pallas import tpu as pltpu

D_MODEL = 1024
D_CONV = 512
CONV_WIDTH = 31
N_HEADS = 8
HEAD_DIM = 64
D_ATTN = N_HEADS * HEAD_DIM
D_MIX = D_CONV + D_ATTN
D_FF = 2816
EPS = 1e-6

V7X_LANES = 128
V7X_SUBLANES = 8
V7X_MXU_DIM = 256
V7X_VMEM_BYTES = 64 * 1024 * 1024

PAIR = 2 * HEAD_DIM
N_PAIRS = N_HEADS // 2
F_PAD = V7X_LANES
N_IN_MAIN = 2 * D_CONV + 3 * D_ATTN
HALO = 32
CONV_ROWS = 32
ONES_ROWS = 16

BF16 = jnp.bfloat16
F32 = jnp.float32


def _plan():
    def hidden_chunks(mxu_tiles):
        step = mxu_tiles * V7X_MXU_DIM
        return tuple((c, min(c + step, D_FF)) for c in range(0, D_FF, step))

    return dict(
        tm=512,
        tq=256,
        ffn_chunks=hidden_chunks(4),
        ffn_in_chunks=hidden_chunks(1),
        vmem_limit=V7X_VMEM_BYTES - 8 * 1024 * 1024,
    )


def _rms(x, g):
    return x * lax.rsqrt(jnp.mean(x * x, axis=-1, keepdims=True) + EPS) * g


def _split3(x):
    hi = x.astype(BF16)
    r = x - hi.astype(F32)
    mid = r.astype(BF16)
    lo = (r - mid.astype(F32)).astype(BF16)
    return hi, mid, lo


def _pack3(x):
    hi, mid, lo = _split3(x)
    packed = (hi.astype(F32) + pltpu.roll(mid.astype(F32), N_HEADS, axis=1)
              + pltpu.roll(lo.astype(F32), 2 * N_HEADS, axis=1))
    return packed.astype(BF16)


def _swiglu(h_bf16, w13_ref, w2_ref, chunks, side_work=None):
    acc = None
    zero = None
    for c, (c0, c1) in enumerate(chunks):
        gate = jnp.dot(h_bf16, w13_ref[:, c0:c1], preferred_element_type=F32)
        up = jnp.dot(h_bf16, w13_ref[:, D_FF + c0:D_FF + c1], preferred_element_type=F32)
        if zero is not None:
            gate = gate + zero
        zero = side_work[c]() if side_work is not None else None
        act = (gate * jax.nn.sigmoid(gate) * up).astype(BF16)
        part = jnp.dot(act, w2_ref[c0:c1, :], preferred_element_type=F32)
        acc = part if acc is None else acc + part
    return acc


def _zero_after(v):
    bits = pltpu.bitcast(v[0:V7X_SUBLANES, 0:V7X_LANES], jnp.uint32)
    bits = lax.shift_right_logical(lax.shift_right_logical(bits, jnp.uint32(16)), jnp.uint32(16))
    return pltpu.bitcast(bits, F32)[0:1, 0:1]


def _conv_rows(u_ref, s_ref, cw_ref, base):
    first = HALO - (CONV_WIDTH - 1)
    groups = CONV_ROWS // V7X_SUBLANES
    acc = None
    for kk in range(CONV_WIDTH):
        aligned, r = divmod(first + kk, V7X_SUBLANES)
        src = u_ref if r == 0 else s_ref.at[r - 1]
        lo = base + aligned * V7X_SUBLANES
        term = src[lo:lo + CONV_ROWS, :].reshape(groups, V7X_SUBLANES, D_CONV) * cw_ref[kk]
        acc = term if acc is None else acc + term
    return acc.reshape(CONV_ROWS, D_CONV)


def _ffn_in_body(x_ref, n1_ref, w13_ref, w2_ref, nm_ref, win_ref, fb_ref, place_ref,
                 cq_ref, ck_ref, cw_ref, cb_ref, lg_ref, lb_ref, on_ref,
                 x1_ref, yc_ref, qt_ref, k_ref, vt_ref, eqt_ref, ek_ref,
                 carry_ref, u_ref, s_ref, cacc_ref, *, chunks, tm, tiles_per_seq):
    n = pl.program_id(0)

    @pl.when(n == 0)
    def _():
        u_ref[...] = jnp.zeros_like(u_ref)
        carry_ref[...] = jnp.zeros_like(carry_ref)

    def tile_sum(v):
        v = v.reshape(-1, V7X_SUBLANES, v.shape[-1]).sum(axis=0)
        return sum(v[:, g:g + V7X_LANES] for g in range(0, v.shape[-1], V7X_LANES))

    def shift_copies():
        for r in range(1, V7X_SUBLANES):
            s_ref[r - 1] = u_ref[r:r + s_ref.shape[1], :]
        return None

    def conv_group(bases):
        def run():
            seen = None
            for base in bases:
                rows = _conv_rows(u_ref, s_ref, cw_ref, base)
                cacc_ref[base:base + CONV_ROWS, :] = rows
                seen = tile_sum(rows) if seen is None else seen + tile_sum(rows)
            return _zero_after(seen)
        return run

    def conv_epilogue():
        cv = cacc_ref[...] + cb_ref[...]
        mu = jnp.mean(cv, axis=-1, keepdims=True)
        xc = cv - mu
        y = xc * lax.rsqrt(jnp.mean(xc * xc, axis=-1, keepdims=True) + EPS) * lg_ref[...] + lb_ref[...]
        y = y * jax.nn.sigmoid(y)
        y = _rms(y, on_ref[...])
        yc_ref[...] = y.astype(BF16)
        return _zero_after(tile_sum(y))

    bases = list(range(0, tm, CONV_ROWS))
    n_groups = len(chunks) - 3
    per_group = -(-len(bases) // n_groups)
    side_work = ([shift_copies]
                 + [conv_group(bases[g * per_group:(g + 1) * per_group]) for g in range(n_groups)]
                 + [conv_epilogue, lambda: None])

    x = x_ref[...]
    h = _rms(x, n1_ref[...]).astype(BF16)
    x1 = x + 0.5 * _swiglu(h, w13_ref, w2_ref, chunks, side_work)
    x1_ref[...] = x1

    h2 = _rms(x1, nm_ref[...]).astype(BF16)
    proj = jnp.dot(h2, win_ref[...], preferred_element_type=F32)

    opens_sequence = lax.rem(n, tiles_per_seq) == 0
    u_ref[0:HALO, :] = jnp.where(opens_sequence, 0.0, u_ref[tm:tm + HALO, :])
    u_ref[HALO:HALO + tm, :] = proj[:, :D_CONV] * jax.nn.sigmoid(proj[:, D_CONV:2 * D_CONV])
    o = 2 * D_CONV
    qt_ref[...] = (proj[:, o:o + D_ATTN] * (HEAD_DIM ** -0.5)).T.astype(BF16)
    k_ref[...] = proj[:, o + D_ATTN:o + 2 * D_ATTN].astype(BF16)
    vt_ref[...] = proj[:, o + 2 * D_ATTN:o + 3 * D_ATTN].T.astype(BF16)

    z = proj[:, N_IN_MAIN:] + fb_ref[...]
    logf = jnp.minimum(z, 0.0) - jnp.log1p(jnp.exp(-jnp.abs(z)))
    lane = lax.broadcasted_iota(jnp.int32, logf.shape, 1)
    logf = jnp.where(lane < N_HEADS, logf, 0.0)

    row = lax.broadcasted_iota(jnp.int32, (tm, tm), 0)
    col = lax.broadcasted_iota(jnp.int32, (tm, tm), 1)
    tril = jnp.where(row >= col, 1.0, 0.0).astype(BF16)
    d_packed = jnp.dot(tril, _pack3(logf), preferred_element_type=F32)
    d_loc = (d_packed + pltpu.roll(d_packed, F_PAD - N_HEADS, axis=1)
             + pltpu.roll(d_packed, F_PAD - 2 * N_HEADS, axis=1))
    d_loc = jnp.where(lane < N_HEADS, d_loc, 0.0)

    d = d_loc + jnp.where(opens_sequence, 0.0, carry_ref[0:1, :])
    carry_ref[0:1, :] = d[tm - 1:tm, :]

    placed = jnp.dot(_pack3(d), place_ref[...], preferred_element_type=F32)
    eqt = (placed[:, :D_ATTN] + cq_ref[...]).T.astype(BF16)
    ekn = (placed[:, D_ATTN:] + ck_ref[...]).astype(BF16)

    @pl.when(n < pl.num_programs(0) - 1)
    def _():
        eqt_ref[...] = eqt
        ek_ref[...] = ekn


def _attn_body(qt_ref, eqt_ref, k_ref, ek_ref, vt_ref, o_ref, ka_ref, va_ref, *, seq, tq):
    ka_ref[:, 0:PAIR] = k_ref[...]
    ka_ref[:, PAIR:2 * PAIR] = ek_ref[...]
    for hh in range(2):
        va_ref[hh, 0:HEAD_DIM, :] = vt_ref[hh * HEAD_DIM:(hh + 1) * HEAD_DIM, :]
        va_ref[hh, HEAD_DIM:HEAD_DIM + ONES_ROWS, :] = jnp.ones((ONES_ROWS, seq), BF16)

    key = lax.broadcasted_iota(jnp.int32, (tq, 2 * tq), 0)
    qry = lax.broadcasted_iota(jnp.int32, (tq, 2 * tq), 1)
    causal = key <= jnp.where(qry < tq, qry, qry - tq)
    zeros = jnp.zeros((HEAD_DIM, tq), BF16)

    def masked_queries(i):
        q, e = qt_ref[:, i * tq:(i + 1) * tq], eqt_ref[:, i * tq:(i + 1) * tq]
        h0 = jnp.concatenate([q[0:HEAD_DIM], zeros, e[0:HEAD_DIM], zeros], axis=0)
        h1 = jnp.concatenate([zeros, q[HEAD_DIM:PAIR], zeros, e[HEAD_DIM:PAIR]], axis=0)
        return jnp.concatenate([h0, h1], axis=1)

    qm = {}

    def scores(i, j):
        if i not in qm:
            qm.clear()
            qm[i] = masked_queries(i)
        s = jnp.dot(ka_ref[j * tq:(j + 1) * tq, :], qm[i], preferred_element_type=F32)
        return jnp.where(causal, s, -jnp.inf) if i == j else s

    tasks = [(i, j) for i in range(seq // tq) for j in range(i + 1)]
    s_next = scores(*tasks[0])
    m = acc = None
    for n, (i, j) in enumerate(tasks):
        s = s_next
        if n + 1 < len(tasks):
            s_next = scores(*tasks[n + 1])
        m_blk = jnp.max(s, axis=0, keepdims=True)
        m_new = m_blk if j == 0 else jnp.maximum(m, m_blk)
        p = jnp.exp(s - m_new).astype(BF16)
        o = [jnp.dot(va_ref[hh, :, j * tq:(j + 1) * tq], p[:, hh * tq:(hh + 1) * tq],
                     preferred_element_type=F32) for hh in range(2)]
        if j == 0:
            acc = o
        else:
            alpha = jnp.exp(m - m_new)
            acc = [acc[hh] * alpha[:, hh * tq:(hh + 1) * tq] + o[hh] for hh in range(2)]
        m = m_new
        if j == i:
            heads = [a[0:HEAD_DIM, :] * (1.0 / a[HEAD_DIM:HEAD_DIM + 1, :]) for a in acc]
            o_ref[i * tq:(i + 1) * tq, :] = jnp.concatenate(heads, axis=0).T


def _out_ffn_body(x1_ref, yc_ref, ya_ref, na_ref, wo_ref, n2_ref, w13_ref, w2_ref, nf_ref,
                  o_ref, *, chunks):
    ya = _rms(ya_ref[...], na_ref[...]).astype(BF16)
    y = (jnp.dot(yc_ref[...], wo_ref[0:D_CONV, :], preferred_element_type=F32)
         + jnp.dot(ya, wo_ref[D_CONV:D_MIX, :], preferred_element_type=F32))
    x2 = x1_ref[...] + y
    h = _rms(x2, n2_ref[...]).astype(BF16)
    x3 = x2 + 0.5 * _swiglu(h, w13_ref, w2_ref, chunks)
    o_ref[...] = _rms(x3, nf_ref[...])


def _resident(shape):
    nd = len(shape)
    return pl.BlockSpec(shape, lambda *_: (0,) * nd, pipeline_mode=pl.Buffered(1))


def _placement():
    place = np.zeros((F_PAD, 2 * D_ATTN), np.float32)
    cq = np.zeros((1, D_ATTN), np.float32)
    ck = np.zeros((1, D_ATTN), np.float32)
    for hd in range(N_HEADS):
        base = hd * HEAD_DIM
        for i in range(3):
            place[i * N_HEADS + hd, base + i] = 1.0
            place[i * N_HEADS + hd, D_ATTN + base + 3 + i] = -1.0
            cq[0, base + 3 + i] = 1.0
            ck[0, base + i] = 1.0
    return jnp.asarray(place, BF16), jnp.asarray(cq), jnp.asarray(ck)


def _layer(x, p, final_norm, plan):
    bsz, seq, _ = x.shape
    tm, tq = plan["tm"], plan["tq"]
    chunks = plan["ffn_chunks"]
    cparams = functools.partial(pltpu.CompilerParams, vmem_limit_bytes=plan["vmem_limit"])
    row2 = lambda v: v.reshape(1, -1)

    w_in = jnp.pad(p["w_in"], ((0, 0), (0, F_PAD - N_HEADS))).astype(BF16)
    fb = jnp.pad(p["forget_b"], (0, F_PAD - N_HEADS)).reshape(1, F_PAD)
    place, cq, ck = _placement()

    tok = lambda width: pl.BlockSpec((None, tm, width), lambda b, s: (b, s, 0))
    n_in = N_IN_MAIN + F_PAD
    tiles_per_seq = seq // tm
    n_tiles = bsz * tiles_per_seq
    cur = lambda n: jnp.minimum(n, n_tiles - 1)
    flat = lambda width: pl.BlockSpec((tm, width), lambda n: (cur(n), 0))
    flat_t = pl.BlockSpec((None, D_ATTN, tm),
                          lambda n: (cur(n) // tiles_per_seq, 0, cur(n) % tiles_per_seq))
    lagged = pl.BlockSpec((tm, D_CONV), lambda n: (jnp.maximum(n - 1, 0), 0))
    attn_t = jax.ShapeDtypeStruct((bsz, D_ATTN, seq), BF16)
    attn_n = jax.ShapeDtypeStruct((bsz * seq, D_ATTN), BF16)
    x1, yc, qt, k, vt, eqt, ek = pl.pallas_call(
        functools.partial(_ffn_in_body, chunks=plan["ffn_in_chunks"], tm=tm,
                          tiles_per_seq=tiles_per_seq),
        grid=(n_tiles + 1,),
        in_specs=[flat(D_MODEL), _resident((1, D_MODEL)), _resident((D_MODEL, 2 * D_FF)),
                  _resident((D_FF, D_MODEL)), _resident((1, D_MODEL)), _resident((D_MODEL, n_in)),
                  _resident((1, F_PAD)), _resident((F_PAD, 2 * D_ATTN)),
                  _resident((1, D_ATTN)), _resident((1, D_ATTN)),
                  _resident((CONV_WIDTH, V7X_SUBLANES, D_CONV))] + [_resident((1, D_CONV))] * 4,
        out_specs=[flat(D_MODEL), lagged, flat_t, flat(D_ATTN), flat_t, flat_t, flat(D_ATTN)],
        out_shape=[jax.ShapeDtypeStruct((bsz * seq, D_MODEL), F32),
                   jax.ShapeDtypeStruct((bsz * seq, D_CONV), BF16),
                   attn_t, attn_n, attn_t, attn_t, attn_n],
        scratch_shapes=[pltpu.VMEM((V7X_SUBLANES, F_PAD), F32),
                        pltpu.VMEM((HALO + tm, D_CONV), F32),
                        pltpu.VMEM((V7X_SUBLANES - 1, HALO + tm - V7X_SUBLANES, D_CONV), F32),
                        pltpu.VMEM((tm, D_CONV), F32)],
        compiler_params=cparams(dimension_semantics=("arbitrary",)),
        name="ffn_in",
    )(x.reshape(bsz * seq, D_MODEL), row2(p["ffn1_norm"]), p["ffn1_w13"].astype(BF16),
      p["ffn1_w2"].astype(BF16), row2(p["mix_norm"]), w_in, fb, place, cq, ck,
      jnp.broadcast_to(p["conv_w"][:, None, :], (CONV_WIDTH, V7X_SUBLANES, D_CONV)),
      row2(p["conv_b"]), row2(p["conv_ln_g"]), row2(p["conv_ln_b"]), row2(p["out_norm_conv"]))
    x1 = x1.reshape(bsz, seq, D_MODEL)
    yc = yc.reshape(bsz, seq, D_CONV)
    k = k.reshape(bsz, seq, D_ATTN)
    ek = ek.reshape(bsz, seq, D_ATTN)

    pair = pl.BlockSpec((None, seq, PAIR), lambda b, g: (b, 0, g))
    pair_t = pl.BlockSpec((None, PAIR, seq), lambda b, g: (b, g, 0))
    ya = pl.pallas_call(
        functools.partial(_attn_body, seq=seq, tq=tq),
        grid=(bsz, N_PAIRS),
        in_specs=[pair_t, pair_t, pair, pair, pair_t],
        out_specs=pair,
        out_shape=jax.ShapeDtypeStruct((bsz, seq, D_ATTN), F32),
        scratch_shapes=[pltpu.VMEM((seq, 2 * PAIR), BF16),
                        pltpu.VMEM((2, HEAD_DIM + ONES_ROWS, seq), BF16)],
        compiler_params=cparams(dimension_semantics=("arbitrary", "arbitrary")),
        name="attn",
    )(qt, eqt, k, ek, vt)

    out = pl.pallas_call(
        functools.partial(_out_ffn_body, chunks=chunks),
        grid=(bsz, seq // tm),
        in_specs=[tok(D_MODEL), tok(D_CONV), tok(D_ATTN), _resident((1, D_ATTN)),
                  _resident((D_MIX, D_MODEL)), _resident((1, D_MODEL)),
                  _resident((D_MODEL, 2 * D_FF)), _resident((D_FF, D_MODEL)),
                  _resident((1, D_MODEL))],
        out_specs=tok(D_MODEL),
        out_shape=jax.ShapeDtypeStruct((bsz, seq, D_MODEL), F32),
        compiler_params=cparams(dimension_semantics=("arbitrary", "arbitrary")),
        name="out_ffn",
    )(x1, yc, ya, row2(p["out_norm_attn"]), p["w_out"].astype(BF16), row2(p["ffn2_norm"]),
      p["ffn2_w13"].astype(BF16), p["ffn2_w2"].astype(BF16), row2(final_norm))
    return out


def kernel(x, ffn1_norm, ffn1_w13, ffn1_w2, mix_norm, w_in, conv_w, conv_b, conv_ln_g, conv_ln_b,
           forget_b, out_norm_conv, out_norm_attn, w_out, ffn2_norm, ffn2_w13, ffn2_w2, final_norm):
    depth = ffn1_norm.shape[0]
    assert depth == 1, "the fused final RMSNorm assumes a single layer"
    plan = _plan()
    stacked = dict(ffn1_norm=ffn1_norm, ffn1_w13=ffn1_w13, ffn1_w2=ffn1_w2, mix_norm=mix_norm,
                   w_in=w_in, conv_w=conv_w, conv_b=conv_b, conv_ln_g=conv_ln_g,
                   conv_ln_b=conv_ln_b, forget_b=forget_b, out_norm_conv=out_norm_conv,
                   out_norm_attn=out_norm_attn, w_out=w_out, ffn2_norm=ffn2_norm,
                   ffn2_w13=ffn2_w13, ffn2_w2=ffn2_w2)
    layer = {name: arr[0] for name, arr in stacked.items()}
    return _layer(x, layer, final_norm, plan)
```

```python
import functools

import numpy as np
import jax
import jax.numpy as jnp
from jax import lax
from jax.experimental import pallas as pl
from jax.experimental.pallas import tpu as pltpu

D_MODEL = 1024
D_CONV = 512
CONV_WIDTH = 31
N_HEADS = 8
HEAD_DIM = 64
D_ATTN = N_HEADS * HEAD_DIM
D_MIX = D_CONV + D_ATTN
D_FF = 2816
EPS = 1e-6
LOG2E = 1.4426950408889634

V7X_LANES = 128
V7X_SUBLANES = 8
V7X_MXU_DIM = 256
V7X_VMEM_BYTES = 64 * 1024 * 1024

PAIR = 2 * HEAD_DIM
N_PAIRS = N_HEADS // 2
F_PAD = V7X_LANES
N_IN_MAIN = 2 * D_CONV + 3 * D_ATTN
HALO = 32
CONV_ROWS = 32
KEY_BLOCKS = 2
ONES_ROWS = 16

BF16 = jnp.bfloat16
F32 = jnp.float32


def _plan():
    def hidden_chunks(mxu_tiles):
        step = mxu_tiles * V7X_MXU_DIM
        return tuple((c, min(c + step, D_FF)) for c in range(0, D_FF, step))

    return dict(
        tm=512,
        tq=256,
        ffn_chunks=hidden_chunks(4),
        ffn_in_chunks=hidden_chunks(1),
        vmem_limit=V7X_VMEM_BYTES - 8 * 1024 * 1024,
    )


def _rms(x, g):
    return x * lax.rsqrt(jnp.mean(x * x, axis=-1, keepdims=True) + EPS) * g


def _split3(x):
    hi = x.astype(BF16)
    r = x - hi.astype(F32)
    mid = r.astype(BF16)
    lo = (r - mid.astype(F32)).astype(BF16)
    return hi, mid, lo


def _pack3(x):
    hi, mid, lo = _split3(x)
    packed = (hi.astype(F32) + pltpu.roll(mid.astype(F32), N_HEADS, axis=1)
              + pltpu.roll(lo.astype(F32), 2 * N_HEADS, axis=1))
    return packed.astype(BF16)


def _swiglu(h_bf16, w13_ref, w2_ref, chunks, side_work=None):
    acc = None
    zero = None
    for c, (c0, c1) in enumerate(chunks):
        gate = jnp.dot(h_bf16, w13_ref[:, c0:c1], preferred_element_type=F32)
        up = jnp.dot(h_bf16, w13_ref[:, D_FF + c0:D_FF + c1], preferred_element_type=F32)
        if zero is not None:
            gate = gate + zero
        zero = side_work[c]() if side_work is not None else None
        act = (gate * jax.nn.sigmoid(gate) * up).astype(BF16)
        part = jnp.dot(act, w2_ref[c0:c1, :], preferred_element_type=F32)
        acc = part if acc is None else acc + part
    return acc


def _zero_after(v):
    bits = pltpu.bitcast(v[0:V7X_SUBLANES, 0:V7X_LANES], jnp.uint32)
    bits = lax.shift_right_logical(lax.shift_right_logical(bits, jnp.uint32(16)), jnp.uint32(16))
    return pltpu.bitcast(bits, F32)[0:1, 0:1]


def _conv_rows(u_ref, s_ref, cw_ref, base):
    first = HALO - (CONV_WIDTH - 1)
    groups = CONV_ROWS // V7X_SUBLANES
    acc = None
    for kk in range(CONV_WIDTH):
        aligned, r = divmod(first + kk, V7X_SUBLANES)
        src = u_ref if r == 0 else s_ref.at[r - 1]
        lo = base + aligned * V7X_SUBLANES
        term = src[lo:lo + CONV_ROWS, :].reshape(groups, V7X_SUBLANES, D_CONV) * cw_ref[kk]
        acc = term if acc is None else acc + term
    return acc.reshape(CONV_ROWS, D_CONV)


def _ffn_in_body(x_ref, n1_ref, w13_ref, w2_ref, nm_ref, win_ref, fb_ref, place_ref,
                 cq_ref, ck_ref, cw_ref, cb_ref, lg_ref, lb_ref, on_ref,
                 x1_ref, yc_ref, qt_ref, k_ref, vt_ref, eqt_ref, ek_ref,
                 carry_ref, u_ref, s_ref, cacc_ref, *, chunks, tm, tiles_per_seq):
    n = pl.program_id(0)

    @pl.when(n == 0)
    def _():
        u_ref[...] = jnp.zeros_like(u_ref)
        carry_ref[...] = jnp.zeros_like(carry_ref)

    def tile_sum(v):
        v = v.reshape(-1, V7X_SUBLANES, v.shape[-1]).sum(axis=0)
        return sum(v[:, g:g + V7X_LANES] for g in range(0, v.shape[-1], V7X_LANES))

    def shift_copies():
        for r in range(1, V7X_SUBLANES):
            s_ref[r - 1] = u_ref[r:r + s_ref.shape[1], :]
        return None

    def conv_group(bases):
        def run():
            seen = None
            for base in bases:
                rows = _conv_rows(u_ref, s_ref, cw_ref, base)
                cacc_ref[base:base + CONV_ROWS, :] = rows
                seen = tile_sum(rows) if seen is None else seen + tile_sum(rows)
            return _zero_after(seen)
        return run

    def conv_epilogue():
        cv = cacc_ref[...] + cb_ref[...]
        mu = jnp.mean(cv, axis=-1, keepdims=True)
        xc = cv - mu
        y = xc * lax.rsqrt(jnp.mean(xc * xc, axis=-1, keepdims=True) + EPS) * lg_ref[...] + lb_ref[...]
        y = y * jax.nn.sigmoid(y)
        y = _rms(y, on_ref[...])
        yc_ref[...] = y.astype(BF16)
        return _zero_after(tile_sum(y))

    bases = list(range(0, tm, CONV_ROWS))
    n_groups = len(chunks) - 3
    per_group = -(-len(bases) // n_groups)
    side_work = ([shift_copies]
                 + [conv_group(bases[g * per_group:(g + 1) * per_group]) for g in range(n_groups)]
                 + [conv_epilogue, lambda: None])

    x = x_ref[...]
    h = _rms(x, n1_ref[...]).astype(BF16)
    x1 = x + 0.5 * _swiglu(h, w13_ref, w2_ref, chunks, side_work)
    x1_ref[...] = x1

    h2 = _rms(x1, nm_ref[...]).astype(BF16)
    proj = jnp.dot(h2, win_ref[...], preferred_element_type=F32)

    opens_sequence = lax.rem(n, tiles_per_seq) == 0
    u_ref[0:HALO, :] = jnp.where(opens_sequence, 0.0, u_ref[tm:tm + HALO, :])
    u_ref[HALO:HALO + tm, :] = proj[:, :D_CONV] * jax.nn.sigmoid(proj[:, D_CONV:2 * D_CONV])
    o = 2 * D_CONV
    qt_ref[...] = (proj[:, o:o + D_ATTN] * (HEAD_DIM ** -0.5 * LOG2E)).T.astype(BF16)
    k_ref[...] = proj[:, o + D_ATTN:o + 2 * D_ATTN].astype(BF16)
    vt_ref[...] = proj[:, o + 2 * D_ATTN:o + 3 * D_ATTN].T.astype(BF16)

    z = proj[:, N_IN_MAIN:] + fb_ref[...]
    logf = jnp.minimum(z, 0.0) - jnp.log1p(jnp.exp(-jnp.abs(z)))
    lane = lax.broadcasted_iota(jnp.int32, logf.shape, 1)
    logf = jnp.where(lane < N_HEADS, logf, 0.0)

    row = lax.broadcasted_iota(jnp.int32, (tm, tm), 0)
    col = lax.broadcasted_iota(jnp.int32, (tm, tm), 1)
    tril = jnp.where(row >= col, 1.0, 0.0).astype(BF16)
    d_packed = jnp.dot(tril, _pack3(logf), preferred_element_type=F32)
    d_loc = (d_packed + pltpu.roll(d_packed, F_PAD - N_HEADS, axis=1)
             + pltpu.roll(d_packed, F_PAD - 2 * N_HEADS, axis=1))
    d_loc = jnp.where(lane < N_HEADS, d_loc, 0.0)

    d = d_loc + jnp.where(opens_sequence, 0.0, carry_ref[0:1, :])
    carry_ref[0:1, :] = d[tm - 1:tm, :]

    placed = jnp.dot(_pack3(d * LOG2E), place_ref[...], preferred_element_type=F32)
    eqt = (placed[:, :D_ATTN] + cq_ref[...]).T.astype(BF16)
    ekn = (placed[:, D_ATTN:] + ck_ref[...]).astype(BF16)

    @pl.when(n < pl.num_programs(0) - 1)
    def _():
        eqt_ref[...] = eqt
        ek_ref[...] = ekn


def _attn_body(qt_ref, eqt_ref, k_ref, ek_ref, vt_ref, o_ref, ka_ref, va_ref, *, seq, tq):
    ka_ref[:, 0:PAIR] = k_ref[...]
    ka_ref[:, PAIR:2 * PAIR] = ek_ref[...]
    for hh in range(2):
        va_ref[hh, 0:HEAD_DIM, :] = vt_ref[hh * HEAD_DIM:(hh + 1) * HEAD_DIM, :]
        va_ref[hh, HEAD_DIM:HEAD_DIM + ONES_ROWS, :] = jnp.ones((ONES_ROWS, seq), BF16)

    key = lax.broadcasted_iota(jnp.int32, (tq, 2 * tq), 0)
    qry = lax.broadcasted_iota(jnp.int32, (tq, 2 * tq), 1)
    causal = key <= jnp.where(qry < tq, qry, qry - tq)
    zeros = jnp.zeros((HEAD_DIM, tq), BF16)

    def masked_queries(i):
        q, e = qt_ref[:, i * tq:(i + 1) * tq], eqt_ref[:, i * tq:(i + 1) * tq]
        h0 = jnp.concatenate([q[0:HEAD_DIM], zeros, e[0:HEAD_DIM], zeros], axis=0)
        h1 = jnp.concatenate([zeros, q[HEAD_DIM:PAIR], zeros, e[HEAD_DIM:PAIR]], axis=0)
        return jnp.concatenate([h0, h1], axis=1)

    qm = {}

    def scores(i, j, w):
        if i not in qm:
            qm.clear()
            qm[i] = masked_queries(i)
        s = jnp.dot(ka_ref[j * tq:(j + w) * tq, :], qm[i], preferred_element_type=F32)
        if j + w <= i:
            return s
        diag = jnp.where(causal, s[(w - 1) * tq:, :], -jnp.inf)
        return diag if w == 1 else jnp.concatenate([s[0:(w - 1) * tq, :], diag], axis=0)

    tasks = [(i, j, min(KEY_BLOCKS, i + 1 - j))
             for i in range(seq // tq) for j in range(0, i + 1, KEY_BLOCKS)]
    s_next = scores(*tasks[0])
    m = acc = None
    for n, (i, j, w) in enumerate(tasks):
        s = s_next
        if n + 1 < len(tasks):
            s_next = scores(*tasks[n + 1])
        m_blk = jnp.max(s, axis=0, keepdims=True)
        m_new = m_blk if j == 0 else jnp.maximum(m, m_blk)
        p = jnp.exp2(s - m_new).astype(BF16)
        o = [jnp.dot(va_ref[hh, :, j * tq:(j + w) * tq], p[:, hh * tq:(hh + 1) * tq],
                     preferred_element_type=F32) for hh in range(2)]
        if j == 0:
            acc = o
        else:
            alpha = jnp.exp2(m - m_new)
            acc = [acc[hh] * alpha[:, hh * tq:(hh + 1) * tq] + o[hh] for hh in range(2)]
        m = m_new
        if j + w == i + 1:
            heads = [a[0:HEAD_DIM, :] * (1.0 / a[HEAD_DIM:HEAD_DIM + 1, :]) for a in acc]
            o_ref[i * tq:(i + 1) * tq, :] = jnp.concatenate(heads, axis=0).T


def _out_ffn_body(x1_ref, yc_ref, ya_ref, na_ref, wo_ref, n2_ref, w13_ref, w2_ref, nf_ref,
                  o_ref, *, chunks):
    ya = _rms(ya_ref[...], na_ref[...]).astype(BF16)
    y = (jnp.dot(yc_ref[...], wo_ref[0:D_CONV, :], preferred_element_type=F32)
         + jnp.dot(ya, wo_ref[D_CONV:D_MIX, :], preferred_element_type=F32))
    x2 = x1_ref[...] + y
    h = _rms(x2, n2_ref[...]).astype(BF16)
    x3 = x2 + 0.5 * _swiglu(h, w13_ref, w2_ref, chunks)
    o_ref[...] = _rms(x3, nf_ref[...])


def _resident(shape):
    nd = len(shape)
    return pl.BlockSpec(shape, lambda *_: (0,) * nd, pipeline_mode=pl.Buffered(1))


def _placement():
    place = np.zeros((F_PAD, 2 * D_ATTN), np.float32)
    cq = np.zeros((1, D_ATTN), np.float32)
    ck = np.zeros((1, D_ATTN), np.float32)
    for hd in range(N_HEADS):
        base = hd * HEAD_DIM
        for i in range(3):
            place[i * N_HEADS + hd, base + i] = 1.0
            place[i * N_HEADS + hd, D_ATTN + base + 3 + i] = -1.0
            cq[0, base + 3 + i] = 1.0
            ck[0, base + i] = 1.0
    return jnp.asarray(place, BF16), jnp.asarray(cq), jnp.asarray(ck)


def _layer(x, p, final_norm, plan):
    bsz, seq, _ = x.shape
    tm, tq = plan["tm"], plan["tq"]
    chunks = plan["ffn_chunks"]
    cparams = functools.partial(pltpu.CompilerParams, vmem_limit_bytes=plan["vmem_limit"])
    row2 = lambda v: v.reshape(1, -1)

    w_in = jnp.pad(p["w_in"], ((0, 0), (0, F_PAD - N_HEADS))).astype(BF16)
    fb = jnp.pad(p["forget_b"], (0, F_PAD - N_HEADS)).reshape(1, F_PAD)
    place, cq, ck = _placement()

    tok = lambda width: pl.BlockSpec((None, tm, width), lambda b, s: (b, s, 0))
    n_in = N_IN_MAIN + F_PAD
    tiles_per_seq = seq // tm
    n_tiles = bsz * tiles_per_seq
    cur = lambda n: jnp.minimum(n, n_tiles - 1)
    flat = lambda width: pl.BlockSpec((tm, width), lambda n: (cur(n), 0))
    flat_t = pl.BlockSpec((None, D_ATTN, tm),
                          lambda n: (cur(n) // tiles_per_seq, 0, cur(n) % tiles_per_seq))
    lagged = pl.BlockSpec((tm, D_CONV), lambda n: (jnp.maximum(n - 1, 0), 0))
    attn_t = jax.ShapeDtypeStruct((bsz, D_ATTN, seq), BF16)
    attn_n = jax.ShapeDtypeStruct((bsz * seq, D_ATTN), BF16)
    x1, yc, qt, k, vt, eqt, ek = pl.pallas_call(
        functools.partial(_ffn_in_body, chunks=plan["ffn_in_chunks"], tm=tm,
                          tiles_per_seq=tiles_per_seq),
        grid=(n_tiles + 1,),
        in_specs=[flat(D_MODEL), _resident((1, D_MODEL)), _resident((D_MODEL, 2 * D_FF)),
                  _resident((D_FF, D_MODEL)), _resident((1, D_MODEL)), _resident((D_MODEL, n_in)),
                  _resident((1, F_PAD)), _resident((F_PAD, 2 * D_ATTN)),
                  _resident((1, D_ATTN)), _resident((1, D_ATTN)),
                  _resident((CONV_WIDTH, V7X_SUBLANES, D_CONV))] + [_resident((1, D_CONV))] * 4,
        out_specs=[flat(D_MODEL), lagged, flat_t, flat(D_ATTN), flat_t, flat_t, flat(D_ATTN)],
        out_shape=[jax.ShapeDtypeStruct((bsz * seq, D_MODEL), F32),
                   jax.ShapeDtypeStruct((bsz * seq, D_CONV), BF16),
                   attn_t, attn_n, attn_t, attn_t, attn_n],
        scratch_shapes=[pltpu.VMEM((V7X_SUBLANES, F_PAD), F32),
                        pltpu.VMEM((HALO + tm, D_CONV), F32),
                        pltpu.VMEM((V7X_SUBLANES - 1, HALO + tm - V7X_SUBLANES, D_CONV), F32),
                        pltpu.VMEM((tm, D_CONV), F32)],
        compiler_params=cparams(dimension_semantics=("arbitrary",)),
        name="ffn_in",
    )(x.reshape(bsz * seq, D_MODEL), row2(p["ffn1_norm"]), p["ffn1_w13"].astype(BF16),
      p["ffn1_w2"].astype(BF16), row2(p["mix_norm"]), w_in, fb, place, cq, ck,
      jnp.broadcast_to(p["conv_w"][:, None, :], (CONV_WIDTH, V7X_SUBLANES, D_CONV)),
      row2(p["conv_b"]), row2(p["conv_ln_g"]), row2(p["conv_ln_b"]), row2(p["out_norm_conv"]))
    x1 = x1.reshape(bsz, seq, D_MODEL)
    yc = yc.reshape(bsz, seq, D_CONV)
    k = k.reshape(bsz, seq, D_ATTN)
    ek = ek.reshape(bsz, seq, D_ATTN)

    pair = pl.BlockSpec((None, seq, PAIR), lambda b, g: (b, 0, g))
    pair_t = pl.BlockSpec((None, PAIR, seq), lambda b, g: (b, g, 0))
    ya = pl.pallas_call(
        functools.partial(_attn_body, seq=seq, tq=tq),
        grid=(bsz, N_PAIRS),
        in_specs=[pair_t, pair_t, pair, pair, pair_t],
        out_specs=pair,
        out_shape=jax.ShapeDtypeStruct((bsz, seq, D_ATTN), F32),
        scratch_shapes=[pltpu.VMEM((seq, 2 * PAIR), BF16),
                        pltpu.VMEM((2, HEAD_DIM + ONES_ROWS, seq), BF16)],
        compiler_params=cparams(dimension_semantics=("arbitrary", "arbitrary")),
        name="attn",
    )(qt, eqt, k, ek, vt)

    out = pl.pallas_call(
        functools.partial(_out_ffn_body, chunks=chunks),
        grid=(bsz, seq // tm),
        in_specs=[tok(D_MODEL), tok(D_CONV), tok(D_ATTN), _resident((1, D_ATTN)),
                  _resident((D_MIX, D_MODEL)), _resident((1, D_MODEL)),
                  _resident((D_MODEL, 2 * D_FF)), _resident((D_FF, D_MODEL)),
                  _resident((1, D_MODEL))],
        out_specs=tok(D_MODEL),
        out_shape=jax.ShapeDtypeStruct((bsz, seq, D_MODEL), F32),
        compiler_params=cparams(dimension_semantics=("arbitrary", "arbitrary")),
        name="out_ffn",
    )(x1, yc, ya, row2(p["out_norm_attn"]), p["w_out"].astype(BF16), row2(p["ffn2_norm"]),
      p["ffn2_w13"].astype(BF16), p["ffn2_w2"].astype(BF16), row2(final_norm))
    return out


def kernel(x, ffn1_norm, ffn1_w13, ffn1_w2, mix_norm, w_in, conv_w, conv_b, conv_ln_g, conv_ln_b,
           forget_b, out_norm_conv, out_norm_attn, w_out, ffn2_norm, ffn2_w13, ffn2_w2, final_norm):
    depth = ffn1_norm.shape[0]
    assert depth == 1, "the fused final RMSNorm assumes a single layer"
    plan = _plan()
    stacked = dict(ffn1_norm=ffn1_norm, ffn1_w13=ffn1_w13, ffn1_w2=ffn1_w2, mix_norm=mix_norm,
                   w_in=w_in, conv_w=conv_w, conv_b=conv_b, conv_ln_g=conv_ln_g,
                   conv_ln_b=conv_ln_b, forget_b=forget_b, out_norm_conv=out_norm_conv,
                   out_norm_attn=out_norm_attn, w_out=w_out, ffn2_norm=ffn2_norm,
                   ffn2_w13=ffn2_w13, ffn2_w2=ffn2_w2)
    layer = {name: arr[0] for name, arr in stacked.items()}
    return _layer(x, layer, final_norm, plan)
```

```python
import functools

import numpy as np
import jax
import jax.numpy as jnp
from jax import lax
from jax.experimental import pallas as pl
from jax.experimental.pallas import tpu as pltpu

D_MODEL = 1024
D_CONV = 512
CONV_WIDTH = 31
N_HEADS = 8
HEAD_DIM = 64
D_ATTN = N_HEADS * HEAD_DIM
D_MIX = D_CONV + D_ATTN
D_FF = 2816
EPS = 1e-6
LOG2E = 1.4426950408889634

V7X_LANES = 128
V7X_SUBLANES = 8
BF16_SUBLANES = 16
V7X_MXU_DIM = 256
V7X_VMEM_BYTES = 64 * 1024 * 1024

PAIR = 2 * HEAD_DIM
N_PAIRS = N_HEADS // 2
F_PAD = V7X_LANES
N_IN_MAIN = 2 * D_CONV + 3 * D_ATTN
HALO = 32
CONV_ROWS = 32
KEY_BLOCKS = 2
ONES_ROWS = 16

BF16 = jnp.bfloat16
F32 = jnp.float32


def _plan():
    def hidden_chunks(mxu_tiles):
        step = mxu_tiles * V7X_MXU_DIM
        return tuple((c, min(c + step, D_FF)) for c in range(0, D_FF, step))

    return dict(
        tm=512,
        tq=256,
        ffn_chunks=hidden_chunks(4),
        ffn_in_chunks=hidden_chunks(1),
        vmem_limit=V7X_VMEM_BYTES - 8 * 1024 * 1024,
    )


def _rms(x, g):
    return x * lax.rsqrt(jnp.mean(x * x, axis=-1, keepdims=True) + EPS) * g


def _split3(x):
    hi = x.astype(BF16)
    r = x - hi.astype(F32)
    mid = r.astype(BF16)
    lo = (r - mid.astype(F32)).astype(BF16)
    return hi, mid, lo


def _pack3(x):
    hi, mid, lo = _split3(x)
    packed = (hi.astype(F32) + pltpu.roll(mid.astype(F32), N_HEADS, axis=1)
              + pltpu.roll(lo.astype(F32), 2 * N_HEADS, axis=1))
    return packed.astype(BF16)


def _swiglu(h_bf16, w13_ref, w2_ref, chunks, side_work=None):
    acc = None
    zero = None
    for c, (c0, c1) in enumerate(chunks):
        gate = jnp.dot(h_bf16, w13_ref[:, c0:c1], preferred_element_type=F32)
        up = jnp.dot(h_bf16, w13_ref[:, D_FF + c0:D_FF + c1], preferred_element_type=F32)
        if zero is not None:
            gate = gate + zero
        zero = side_work[c]() if side_work is not None else None
        act = (gate * jax.nn.sigmoid(gate) * up).astype(BF16)
        part = jnp.dot(act, w2_ref[c0:c1, :], preferred_element_type=F32)
        acc = part if acc is None else acc + part
    return acc


def _zero_after(v):
    bits = pltpu.bitcast(v[0:V7X_SUBLANES, 0:V7X_LANES], jnp.uint32)
    bits = lax.shift_right_logical(lax.shift_right_logical(bits, jnp.uint32(16)), jnp.uint32(16))
    return pltpu.bitcast(bits, F32)[0:1, 0:1]


def _conv_rows(u_ref, s_ref, cw_ref, base):
    first = HALO - (CONV_WIDTH - 1)
    groups = CONV_ROWS // V7X_SUBLANES
    acc = None
    for kk in range(CONV_WIDTH):
        aligned, r = divmod(first + kk, V7X_SUBLANES)
        src = u_ref if r == 0 else s_ref.at[r - 1]
        lo = base + aligned * V7X_SUBLANES
        term = src[lo:lo + CONV_ROWS, :].reshape(groups, V7X_SUBLANES, D_CONV) * cw_ref[kk]
        acc = term if acc is None else acc + term
    return acc.reshape(CONV_ROWS, D_CONV)


def _ffn_in_body(x_ref, n1_ref, w13_ref, w2_ref, nm_ref, win_ref, fb_ref, place_ref,
                 cq_ref, ck_ref, cw_ref, cb_ref, lg_ref, lb_ref, on_ref,
                 x1_ref, yc_ref, qt_ref, k_ref, vt_ref, eqt_ref, ek_ref,
                 carry_ref, u_ref, s_ref, cacc_ref, *, chunks, tm, tiles_per_seq):
    n = pl.program_id(0)

    @pl.when(n == 0)
    def _():
        u_ref[...] = jnp.zeros_like(u_ref)
        carry_ref[...] = jnp.zeros_like(carry_ref)

    def tile_sum(v):
        v = v.reshape(-1, V7X_SUBLANES, v.shape[-1]).sum(axis=0)
        return sum(v[:, g:g + V7X_LANES] for g in range(0, v.shape[-1], V7X_LANES))

    def shift_copies():
        for r in range(1, V7X_SUBLANES):
            s_ref[r - 1] = u_ref[r:r + s_ref.shape[1], :]
        return None

    def conv_group(bases):
        def run():
            seen = None
            for base in bases:
                rows = _conv_rows(u_ref, s_ref, cw_ref, base)
                cacc_ref[base:base + CONV_ROWS, :] = rows
                seen = tile_sum(rows) if seen is None else seen + tile_sum(rows)
            return _zero_after(seen)
        return run

    def conv_epilogue():
        cv = cacc_ref[...] + cb_ref[...]
        mu = jnp.mean(cv, axis=-1, keepdims=True)
        xc = cv - mu
        y = xc * lax.rsqrt(jnp.mean(xc * xc, axis=-1, keepdims=True) + EPS) * lg_ref[...] + lb_ref[...]
        y = y * jax.nn.sigmoid(y)
        y = _rms(y, on_ref[...])
        yc_ref[...] = y.astype(BF16)
        return _zero_after(tile_sum(y))

    bases = list(range(0, tm, CONV_ROWS))
    n_groups = len(chunks) - 3
    per_group = -(-len(bases) // n_groups)
    side_work = ([shift_copies]
                 + [conv_group(bases[g * per_group:(g + 1) * per_group]) for g in range(n_groups)]
                 + [conv_epilogue, lambda: None])

    x = x_ref[...]
    h = _rms(x, n1_ref[...]).astype(BF16)
    x1 = x + 0.5 * _swiglu(h, w13_ref, w2_ref, chunks, side_work)
    x1_ref[...] = x1

    h2 = _rms(x1, nm_ref[...]).astype(BF16)
    proj = jnp.dot(h2, win_ref[...], preferred_element_type=F32)

    opens_sequence = lax.rem(n, tiles_per_seq) == 0
    u_ref[0:HALO, :] = jnp.where(opens_sequence, 0.0, u_ref[tm:tm + HALO, :])
    u_ref[HALO:HALO + tm, :] = proj[:, :D_CONV] * jax.nn.sigmoid(proj[:, D_CONV:2 * D_CONV])
    o = 2 * D_CONV
    qt_ref[...] = (proj[:, o:o + D_ATTN] * (HEAD_DIM ** -0.5 * LOG2E)).T.astype(BF16)
    k_ref[...] = proj[:, o + D_ATTN:o + 2 * D_ATTN].astype(BF16)
    vt_ref[...] = proj[:, o + 2 * D_ATTN:o + 3 * D_ATTN].T.astype(BF16)

    z = proj[:, N_IN_MAIN:] + fb_ref[...]
    logf = jnp.minimum(z, 0.0) - jnp.log1p(jnp.exp(-jnp.abs(z)))
    lane = lax.broadcasted_iota(jnp.int32, logf.shape, 1)
    logf = jnp.where(lane < N_HEADS, logf, 0.0)

    row = lax.broadcasted_iota(jnp.int32, (tm, tm), 0)
    col = lax.broadcasted_iota(jnp.int32, (tm, tm), 1)
    tril = jnp.where(row >= col, 1.0, 0.0).astype(BF16)
    d_packed = jnp.dot(tril, _pack3(logf), preferred_element_type=F32)
    d_loc = (d_packed + pltpu.roll(d_packed, F_PAD - N_HEADS, axis=1)
             + pltpu.roll(d_packed, F_PAD - 2 * N_HEADS, axis=1))
    d_loc = jnp.where(lane < N_HEADS, d_loc, 0.0)

    d = d_loc + jnp.where(opens_sequence, 0.0, carry_ref[0:1, :])
    carry_ref[0:1, :] = d[tm - 1:tm, :]

    placed = jnp.dot(_pack3(d * LOG2E), place_ref[...], preferred_element_type=F32)
    eqt = (placed[:, :D_ATTN] + cq_ref[...]).T.astype(BF16)
    ekn = (placed[:, D_ATTN:] + ck_ref[...]).astype(BF16)

    @pl.when(n < pl.num_programs(0) - 1)
    def _():
        eqt_ref[...] = eqt
        ek_ref[...] = ekn


def _attn_body(qt_ref, eqt_ref, k_ref, ek_ref, vt_ref, wa_ref, wb_ref, wc_ref,
               o_ref, wa_out, wb_out, wc_out, ka_ref, va_ref, *, seq, tq):
    for w_in_ref, w_out_ref in ((wa_ref, wa_out), (wb_ref, wb_out), (wc_ref, wc_out)):
        w_out_ref[...] = w_in_ref[...].astype(BF16)

    ka_ref[:, 0:PAIR] = k_ref[...]
    ka_ref[:, PAIR:2 * PAIR] = ek_ref[...]
    for hh in range(2):
        va_ref[hh, 0:HEAD_DIM, :] = vt_ref[hh * HEAD_DIM:(hh + 1) * HEAD_DIM, :]
        va_ref[hh, HEAD_DIM:HEAD_DIM + ONES_ROWS, :] = jnp.ones((ONES_ROWS, seq), BF16)

    key = lax.broadcasted_iota(jnp.int32, (tq, 2 * tq), 0)
    qry = lax.broadcasted_iota(jnp.int32, (tq, 2 * tq), 1)
    causal = key <= jnp.where(qry < tq, qry, qry - tq)
    zeros = jnp.zeros((HEAD_DIM, tq), BF16)

    def masked_queries(i):
        q, e = qt_ref[:, i * tq:(i + 1) * tq], eqt_ref[:, i * tq:(i + 1) * tq]
        h0 = jnp.concatenate([q[0:HEAD_DIM], zeros, e[0:HEAD_DIM], zeros], axis=0)
        h1 = jnp.concatenate([zeros, q[HEAD_DIM:PAIR], zeros, e[HEAD_DIM:PAIR]], axis=0)
        return jnp.concatenate([h0, h1], axis=1)

    qm = {}

    def scores(i, j, w):
        if i not in qm:
            qm.clear()
            qm[i] = masked_queries(i)
        s = jnp.dot(ka_ref[j * tq:(j + w) * tq, :], qm[i], preferred_element_type=F32)
        if j + w <= i:
            return s
        diag = jnp.where(causal, s[(w - 1) * tq:, :], -jnp.inf)
        return diag if w == 1 else jnp.concatenate([s[0:(w - 1) * tq, :], diag], axis=0)

    tasks = [(i, j, min(KEY_BLOCKS, i + 1 - j))
             for i in range(seq // tq) for j in range(0, i + 1, KEY_BLOCKS)]
    s_next = scores(*tasks[0])
    m = acc = None
    for n, (i, j, w) in enumerate(tasks):
        s = s_next
        if n + 1 < len(tasks):
            s_next = scores(*tasks[n + 1])
        m_blk = jnp.max(s, axis=0, keepdims=True)
        m_new = m_blk if j == 0 else jnp.maximum(m, m_blk)
        p = jnp.exp2(s - m_new).astype(BF16)
        o = [jnp.dot(va_ref[hh, :, j * tq:(j + w) * tq], p[:, hh * tq:(hh + 1) * tq],
                     preferred_element_type=F32) for hh in range(2)]
        if j == 0:
            acc = o
        else:
            alpha = jnp.exp2(m - m_new)
            acc = [acc[hh] * alpha[:, hh * tq:(hh + 1) * tq] + o[hh] for hh in range(2)]
        m = m_new
        if j + w == i + 1:
            heads = [a[0:HEAD_DIM, :] * (1.0 / a[HEAD_DIM:HEAD_DIM + 1, :]) for a in acc]
            o_ref[i * tq:(i + 1) * tq, :] = jnp.concatenate(heads, axis=0).T


def _out_ffn_body(x1_ref, yc_ref, ya_ref, na_ref, wo_ref, n2_ref, w13_ref, w2_ref, nf_ref,
                  o_ref, *, chunks):
    ya = _rms(ya_ref[...], na_ref[...]).astype(BF16)
    y = (jnp.dot(yc_ref[...], wo_ref[0:D_CONV, :], preferred_element_type=F32)
         + jnp.dot(ya, wo_ref[D_CONV:D_MIX, :], preferred_element_type=F32))
    x2 = x1_ref[...] + y
    h = _rms(x2, n2_ref[...]).astype(BF16)
    x3 = x2 + 0.5 * _swiglu(h, w13_ref, w2_ref, chunks)
    o_ref[...] = _rms(x3, nf_ref[...])


def _resident(shape):
    nd = len(shape)
    return pl.BlockSpec(shape, lambda *_: (0,) * nd, pipeline_mode=pl.Buffered(1))


def _placement():
    place = np.zeros((F_PAD, 2 * D_ATTN), np.float32)
    cq = np.zeros((1, D_ATTN), np.float32)
    ck = np.zeros((1, D_ATTN), np.float32)
    for hd in range(N_HEADS):
        base = hd * HEAD_DIM
        for i in range(3):
            place[i * N_HEADS + hd, base + i] = 1.0
            place[i * N_HEADS + hd, D_ATTN + base + 3 + i] = -1.0
            cq[0, base + 3 + i] = 1.0
            ck[0, base + i] = 1.0
    return jnp.asarray(place, BF16), jnp.asarray(cq), jnp.asarray(ck)


def _layer(x, p, final_norm, plan):
    bsz, seq, _ = x.shape
    tm, tq = plan["tm"], plan["tq"]
    chunks = plan["ffn_chunks"]
    cparams = functools.partial(pltpu.CompilerParams, vmem_limit_bytes=plan["vmem_limit"])
    row2 = lambda v: v.reshape(1, -1)

    w_in = jnp.pad(p["w_in"], ((0, 0), (0, F_PAD - N_HEADS))).astype(BF16)
    fb = jnp.pad(p["forget_b"], (0, F_PAD - N_HEADS)).reshape(1, F_PAD)
    place, cq, ck = _placement()

    tok = lambda width: pl.BlockSpec((None, tm, width), lambda b, s: (b, s, 0))
    n_in = N_IN_MAIN + F_PAD
    tiles_per_seq = seq // tm
    n_tiles = bsz * tiles_per_seq
    cur = lambda n: jnp.minimum(n, n_tiles - 1)
    flat = lambda width: pl.BlockSpec((tm, width), lambda n: (cur(n), 0))
    flat_t = pl.BlockSpec((None, D_ATTN, tm),
                          lambda n: (cur(n) // tiles_per_seq, 0, cur(n) % tiles_per_seq))
    lagged = pl.BlockSpec((tm, D_CONV), lambda n: (jnp.maximum(n - 1, 0), 0))
    attn_t = jax.ShapeDtypeStruct((bsz, D_ATTN, seq), BF16)
    attn_n = jax.ShapeDtypeStruct((bsz * seq, D_ATTN), BF16)
    x1, yc, qt, k, vt, eqt, ek = pl.pallas_call(
        functools.partial(_ffn_in_body, chunks=plan["ffn_in_chunks"], tm=tm,
                          tiles_per_seq=tiles_per_seq),
        grid=(n_tiles + 1,),
        in_specs=[flat(D_MODEL), _resident((1, D_MODEL)), _resident((D_MODEL, 2 * D_FF)),
                  _resident((D_FF, D_MODEL)), _resident((1, D_MODEL)), _resident((D_MODEL, n_in)),
                  _resident((1, F_PAD)), _resident((F_PAD, 2 * D_ATTN)),
                  _resident((1, D_ATTN)), _resident((1, D_ATTN)),
                  _resident((CONV_WIDTH, V7X_SUBLANES, D_CONV))] + [_resident((1, D_CONV))] * 4,
        out_specs=[flat(D_MODEL), lagged, flat_t, flat(D_ATTN), flat_t, flat_t, flat(D_ATTN)],
        out_shape=[jax.ShapeDtypeStruct((bsz * seq, D_MODEL), F32),
                   jax.ShapeDtypeStruct((bsz * seq, D_CONV), BF16),
                   attn_t, attn_n, attn_t, attn_t, attn_n],
        scratch_shapes=[pltpu.VMEM((V7X_SUBLANES, F_PAD), F32),
                        pltpu.VMEM((HALO + tm, D_CONV), F32),
                        pltpu.VMEM((V7X_SUBLANES - 1, HALO + tm - V7X_SUBLANES, D_CONV), F32),
                        pltpu.VMEM((tm, D_CONV), F32)],
        compiler_params=cparams(dimension_semantics=("arbitrary",)),
        name="ffn_in",
    )(x.reshape(bsz * seq, D_MODEL), row2(p["ffn1_norm"]), p["ffn1_w13"].astype(BF16),
      p["ffn1_w2"].astype(BF16), row2(p["mix_norm"]), w_in, fb, place, cq, ck,
      jnp.broadcast_to(p["conv_w"][:, None, :], (CONV_WIDTH, V7X_SUBLANES, D_CONV)),
      row2(p["conv_b"]), row2(p["conv_ln_g"]), row2(p["conv_ln_b"]), row2(p["out_norm_conv"]))
    x1 = x1.reshape(bsz, seq, D_MODEL)
    yc = yc.reshape(bsz, seq, D_CONV)
    k = k.reshape(bsz, seq, D_ATTN)
    ek = ek.reshape(bsz, seq, D_ATTN)

    pair = pl.BlockSpec((None, seq, PAIR), lambda b, g: (b, 0, g))
    pair_t = pl.BlockSpec((None, PAIR, seq), lambda b, g: (b, g, 0))
    steps = bsz * N_PAIRS

    def n_slabs(w):
        return max(d for d in range(1, steps + 1)
                   if steps % d == 0 and w.shape[0] % (d * BF16_SUBLANES) == 0)

    slab = lambda w: w.reshape(n_slabs(w), w.shape[0] // n_slabs(w), w.shape[1])
    slab_spec = lambda w: pl.BlockSpec(
        (None,) + slab(w).shape[1:],
        lambda b, g, per=steps // n_slabs(w): ((b * N_PAIRS + g) // per, 0, 0))
    late_w = [p["w_out"], p["ffn2_w13"], p["ffn2_w2"]]
    ya, w_out, w13_2, w2_2 = pl.pallas_call(
        functools.partial(_attn_body, seq=seq, tq=tq),
        grid=(bsz, N_PAIRS),
        in_specs=[pair_t, pair_t, pair, pair, pair_t] + [slab_spec(w) for w in late_w],
        out_specs=[pair] + [slab_spec(w) for w in late_w],
        out_shape=[jax.ShapeDtypeStruct((bsz, seq, D_ATTN), F32)]
                  + [jax.ShapeDtypeStruct(slab(w).shape, BF16) for w in late_w],
        scratch_shapes=[pltpu.VMEM((seq, 2 * PAIR), BF16),
                        pltpu.VMEM((2, HEAD_DIM + ONES_ROWS, seq), BF16)],
        compiler_params=cparams(dimension_semantics=("arbitrary", "arbitrary")),
        name="attn",
    )(qt, eqt, k, ek, vt, *[slab(w) for w in late_w])
    w_out, w13_2, w2_2 = [w.reshape(src.shape) for w, src in zip((w_out, w13_2, w2_2), late_w)]

    out = pl.pallas_call(
        functools.partial(_out_ffn_body, chunks=chunks),
        grid=(bsz, seq // tm),
        in_specs=[tok(D_MODEL), tok(D_CONV), tok(D_ATTN), _resident((1, D_ATTN)),
                  _resident((D_MIX, D_MODEL)), _resident((1, D_MODEL)),
                  _resident((D_MODEL, 2 * D_FF)), _resident((D_FF, D_MODEL)),
                  _resident((1, D_MODEL))],
        out_specs=tok(D_MODEL),
        out_shape=jax.ShapeDtypeStruct((bsz, seq, D_MODEL), F32),
        compiler_params=cparams(dimension_semantics=("arbitrary", "arbitrary")),
        name="out_ffn",
    )(x1, yc, ya, row2(p["out_norm_attn"]), w_out, row2(p["ffn2_norm"]), w13_2, w2_2,
      row2(final_norm))
    return out


def kernel(x, ffn1_norm, ffn1_w13, ffn1_w2, mix_norm, w_in, conv_w, conv_b, conv_ln_g, conv_ln_b,
           forget_b, out_norm_conv, out_norm_attn, w_out, ffn2_norm, ffn2_w13, ffn2_w2, final_norm):
    depth = ffn1_norm.shape[0]
    assert depth == 1, "the fused final RMSNorm assumes a single layer"
    plan = _plan()
    stacked = dict(ffn1_norm=ffn1_norm, ffn1_w13=ffn1_w13, ffn1_w2=ffn1_w2, mix_norm=mix_norm,
                   w_in=w_in, conv_w=conv_w, conv_b=conv_b, conv_ln_g=conv_ln_g,
                   conv_ln_b=conv_ln_b, forget_b=forget_b, out_norm_conv=out_norm_conv,
                   out_norm_attn=out_norm_attn, w_out=w_out, ffn2_norm=ffn2_norm,
                   ffn2_w13=ffn2_w13, ffn2_w2=ffn2_w2)
    layer = {name: arr[0] for name, arr in stacked.items()}
    return _layer(x, layer, final_norm, plan)
```

```python
import functools

import numpy as np
import jax
import jax.numpy as jnp
from jax import lax
from jax.experimental import pallas as pl
from jax.experimental.pallas import tpu as pltpu

D_MODEL = 1024
D_CONV = 512
CONV_WIDTH = 31
N_HEADS = 8
HEAD_DIM = 64
D_ATTN = N_HEADS * HEAD_DIM
D_MIX = D_CONV + D_ATTN
D_FF = 2816
EPS = 1e-6
LOG2E = 1.4426950408889634

V7X_LANES = 128
V7X_SUBLANES = 8
BF16_SUBLANES = 16
V7X_MXU_DIM = 256
V7X_VMEM_BYTES = 64 * 1024 * 1024

PAIR = 2 * HEAD_DIM
N_PAIRS = N_HEADS // 2
F_PAD = V7X_LANES
N_IN_MAIN = 2 * D_CONV + 3 * D_ATTN
HALO = 32
CONV_ROWS = 32
KEY_BLOCKS = 2
ONES_ROWS = 16

BF16 = jnp.bfloat16
F32 = jnp.float32


def _plan():
    def hidden_chunks(mxu_tiles):
        step = mxu_tiles * V7X_MXU_DIM
        return tuple((c, min(c + step, D_FF)) for c in range(0, D_FF, step))

    return dict(
        tm=512,
        tm_out=1024,
        tq=256,
        ffn_chunks=hidden_chunks(6),
        ffn_in_chunks=hidden_chunks(1),
        vmem_limit=V7X_VMEM_BYTES - 8 * 1024 * 1024,
    )


def _rms(x, g):
    return x * lax.rsqrt(jnp.mean(x * x, axis=-1, keepdims=True) + EPS) * g


def _split3(x):
    hi = x.astype(BF16)
    r = x - hi.astype(F32)
    mid = r.astype(BF16)
    lo = (r - mid.astype(F32)).astype(BF16)
    return hi, mid, lo


def _pack3(x):
    hi, mid, lo = _split3(x)
    packed = (hi.astype(F32) + pltpu.roll(mid.astype(F32), N_HEADS, axis=1)
              + pltpu.roll(lo.astype(F32), 2 * N_HEADS, axis=1))
    return packed.astype(BF16)


def _swiglu(h_bf16, w13_ref, w2_ref, chunks, side_work=None):
    acc = None
    zero = None
    for c, (c0, c1) in enumerate(chunks):
        gate = jnp.dot(h_bf16, w13_ref[:, c0:c1], preferred_element_type=F32)
        up = jnp.dot(h_bf16, w13_ref[:, D_FF + c0:D_FF + c1], preferred_element_type=F32)
        if zero is not None:
            gate = gate + zero
        zero = side_work[c]() if side_work is not None else None
        act = (gate * jax.nn.sigmoid(gate) * up).astype(BF16)
        part = jnp.dot(act, w2_ref[c0:c1, :], preferred_element_type=F32)
        acc = part if acc is None else acc + part
    return acc


def _zero_after(v):
    bits = pltpu.bitcast(v[0:V7X_SUBLANES, 0:V7X_LANES], jnp.uint32)
    bits = lax.shift_right_logical(lax.shift_right_logical(bits, jnp.uint32(16)), jnp.uint32(16))
    return pltpu.bitcast(bits, F32)[0:1, 0:1]


def _conv_rows(u_ref, s_ref, cw_ref, base):
    first = HALO - (CONV_WIDTH - 1)
    groups = CONV_ROWS // V7X_SUBLANES
    acc = None
    for kk in range(CONV_WIDTH):
        aligned, r = divmod(first + kk, V7X_SUBLANES)
        src = u_ref if r == 0 else s_ref.at[r - 1]
        lo = base + aligned * V7X_SUBLANES
        term = src[lo:lo + CONV_ROWS, :].reshape(groups, V7X_SUBLANES, D_CONV) * cw_ref[kk]
        acc = term if acc is None else acc + term
    return acc.reshape(CONV_ROWS, D_CONV)


def _ffn_in_body(x_ref, n1_ref, w13_ref, w2_ref, nm_ref, win_ref, fb_ref, place_ref,
                 cq_ref, ck_ref, cw_ref, cb_ref, lg_ref, lb_ref, on_ref,
                 x1_ref, yc_ref, qt_ref, k_ref, vt_ref, eqt_ref, ek_ref,
                 carry_ref, u_ref, s_ref, cacc_ref, *, chunks, tm, tiles_per_seq):
    n = pl.program_id(0)

    @pl.when(n == 0)
    def _():
        u_ref[...] = jnp.zeros_like(u_ref)
        carry_ref[...] = jnp.zeros_like(carry_ref)

    def tile_sum(v):
        v = v.reshape(-1, V7X_SUBLANES, v.shape[-1]).sum(axis=0)
        return sum(v[:, g:g + V7X_LANES] for g in range(0, v.shape[-1], V7X_LANES))

    def shift_copies():
        for r in range(1, V7X_SUBLANES):
            s_ref[r - 1] = u_ref[r:r + s_ref.shape[1], :]
        return None

    def conv_group(bases):
        def run():
            seen = None
            for base in bases:
                rows = _conv_rows(u_ref, s_ref, cw_ref, base)
                cacc_ref[base:base + CONV_ROWS, :] = rows
                seen = tile_sum(rows) if seen is None else seen + tile_sum(rows)
            return _zero_after(seen)
        return run

    def conv_epilogue():
        cv = cacc_ref[...] + cb_ref[...]
        mu = jnp.mean(cv, axis=-1, keepdims=True)
        xc = cv - mu
        y = xc * lax.rsqrt(jnp.mean(xc * xc, axis=-1, keepdims=True) + EPS) * lg_ref[...] + lb_ref[...]
        y = y * jax.nn.sigmoid(y)
        y = _rms(y, on_ref[...])
        yc_ref[...] = y.astype(BF16)
        return _zero_after(tile_sum(y))

    bases = list(range(0, tm, CONV_ROWS))
    n_groups = len(chunks) - 3
    per_group = -(-len(bases) // n_groups)
    side_work = ([shift_copies]
                 + [conv_group(bases[g * per_group:(g + 1) * per_group]) for g in range(n_groups)]
                 + [conv_epilogue, lambda: None])

    x = x_ref[...]
    h = _rms(x, n1_ref[...]).astype(BF16)
    x1 = x + 0.5 * _swiglu(h, w13_ref, w2_ref, chunks, side_work)
    x1_ref[...] = x1

    h2 = _rms(x1, nm_ref[...]).astype(BF16)
    proj = jnp.dot(h2, win_ref[...], preferred_element_type=F32)

    opens_sequence = lax.rem(n, tiles_per_seq) == 0
    u_ref[0:HALO, :] = jnp.where(opens_sequence, 0.0, u_ref[tm:tm + HALO, :])
    u_ref[HALO:HALO + tm, :] = proj[:, :D_CONV] * jax.nn.sigmoid(proj[:, D_CONV:2 * D_CONV])
    o = 2 * D_CONV
    qt_ref[...] = (proj[:, o:o + D_ATTN] * (HEAD_DIM ** -0.5 * LOG2E)).T.astype(BF16)
    k_ref[...] = proj[:, o + D_ATTN:o + 2 * D_ATTN].astype(BF16)
    vt_ref[...] = proj[:, o + 2 * D_ATTN:o + 3 * D_ATTN].T.astype(BF16)

    z = proj[:, N_IN_MAIN:] + fb_ref[...]
    logf = jnp.minimum(z, 0.0) - jnp.log1p(jnp.exp(-jnp.abs(z)))
    lane = lax.broadcasted_iota(jnp.int32, logf.shape, 1)
    logf = jnp.where(lane < N_HEADS, logf, 0.0)

    row = lax.broadcasted_iota(jnp.int32, (tm, tm), 0)
    col = lax.broadcasted_iota(jnp.int32, (tm, tm), 1)
    tril = jnp.where(row >= col, 1.0, 0.0).astype(BF16)
    d_packed = jnp.dot(tril, _pack3(logf), preferred_element_type=F32)
    d_loc = (d_packed + pltpu.roll(d_packed, F_PAD - N_HEADS, axis=1)
             + pltpu.roll(d_packed, F_PAD - 2 * N_HEADS, axis=1))
    d_loc = jnp.where(lane < N_HEADS, d_loc, 0.0)

    d = d_loc + jnp.where(opens_sequence, 0.0, carry_ref[0:1, :])
    carry_ref[0:1, :] = d[tm - 1:tm, :]

    placed = jnp.dot(_pack3(d * LOG2E), place_ref[...], preferred_element_type=F32)
    eqt = (placed[:, :D_ATTN] + cq_ref[...]).T.astype(BF16)
    ekn = (placed[:, D_ATTN:] + ck_ref[...]).astype(BF16)

    @pl.when(n < pl.num_programs(0) - 1)
    def _():
        eqt_ref[...] = eqt
        ek_ref[...] = ekn


def _attn_body(qt_ref, eqt_ref, k_ref, ek_ref, vt_ref, wa_ref, wb_ref, wc_ref,
               o_ref, wa_out, wb_out, wc_out, ka_ref, va_ref, *, seq, tq):
    for w_in_ref, w_out_ref in ((wa_ref, wa_out), (wb_ref, wb_out), (wc_ref, wc_out)):
        w_out_ref[...] = w_in_ref[...].astype(BF16)

    ka_ref[:, 0:PAIR] = k_ref[...]
    ka_ref[:, PAIR:2 * PAIR] = ek_ref[...]
    for hh in range(2):
        va_ref[hh, 0:HEAD_DIM, :] = vt_ref[hh * HEAD_DIM:(hh + 1) * HEAD_DIM, :]
        va_ref[hh, HEAD_DIM:HEAD_DIM + ONES_ROWS, :] = jnp.ones((ONES_ROWS, seq), BF16)

    key = lax.broadcasted_iota(jnp.int32, (tq, 2 * tq), 0)
    qry = lax.broadcasted_iota(jnp.int32, (tq, 2 * tq), 1)
    causal = key <= jnp.where(qry < tq, qry, qry - tq)
    zeros = jnp.zeros((HEAD_DIM, tq), BF16)

    def masked_queries(i):
        q, e = qt_ref[:, i * tq:(i + 1) * tq], eqt_ref[:, i * tq:(i + 1) * tq]
        h0 = jnp.concatenate([q[0:HEAD_DIM], zeros, e[0:HEAD_DIM], zeros], axis=0)
        h1 = jnp.concatenate([zeros, q[HEAD_DIM:PAIR], zeros, e[HEAD_DIM:PAIR]], axis=0)
        return jnp.concatenate([h0, h1], axis=1)

    qm = {}

    def scores(i, j, w):
        if i not in qm:
            qm.clear()
            qm[i] = masked_queries(i)
        s = jnp.dot(ka_ref[j * tq:(j + w) * tq, :], qm[i], preferred_element_type=F32)
        if j + w <= i:
            return s
        diag = jnp.where(causal, s[(w - 1) * tq:, :], -jnp.inf)
        return diag if w == 1 else jnp.concatenate([s[0:(w - 1) * tq, :], diag], axis=0)

    tasks = [(i, j, min(KEY_BLOCKS, i + 1 - j))
             for i in range(seq // tq) for j in range(0, i + 1, KEY_BLOCKS)]
    s_next = scores(*tasks[0])
    m = acc = None
    for n, (i, j, w) in enumerate(tasks):
        s = s_next
        if n + 1 < len(tasks):
            s_next = scores(*tasks[n + 1])
        m_blk = jnp.max(s, axis=0, keepdims=True)
        m_new = m_blk if j == 0 else jnp.maximum(m, m_blk)
        p = jnp.exp2(s - m_new).astype(BF16)
        o = [jnp.dot(va_ref[hh, :, j * tq:(j + w) * tq], p[:, hh * tq:(hh + 1) * tq],
                     preferred_element_type=F32) for hh in range(2)]
        if j == 0:
            acc = o
        else:
            alpha = jnp.exp2(m - m_new)
            acc = [acc[hh] * alpha[:, hh * tq:(hh + 1) * tq] + o[hh] for hh in range(2)]
        m = m_new
        if j + w == i + 1:
            heads = [a[0:HEAD_DIM, :] * (1.0 / a[HEAD_DIM:HEAD_DIM + 1, :]) for a in acc]
            o_ref[i * tq:(i + 1) * tq, :] = jnp.concatenate(heads, axis=0).T


def _out_ffn_body(x1_ref, yc_ref, ya_ref, na_ref, wo_ref, n2_ref, w13_ref, w2_ref, nf_ref,
                  o_ref, *, chunks):
    ya = _rms(ya_ref[...], na_ref[...]).astype(BF16)
    y = (jnp.dot(yc_ref[...], wo_ref[0:D_CONV, :], preferred_element_type=F32)
         + jnp.dot(ya, wo_ref[D_CONV:D_MIX, :], preferred_element_type=F32))
    x2 = x1_ref[...] + y
    h = _rms(x2, n2_ref[...]).astype(BF16)
    x3 = x2 + 0.5 * _swiglu(h, w13_ref, w2_ref, chunks)
    o_ref[...] = _rms(x3, nf_ref[...])


def _resident(shape):
    nd = len(shape)
    return pl.BlockSpec(shape, lambda *_: (0,) * nd, pipeline_mode=pl.Buffered(1))


def _placement():
    place = np.zeros((F_PAD, 2 * D_ATTN), np.float32)
    cq = np.zeros((1, D_ATTN), np.float32)
    ck = np.zeros((1, D_ATTN), np.float32)
    for hd in range(N_HEADS):
        base = hd * HEAD_DIM
        for i in range(3):
            place[i * N_HEADS + hd, base + i] = 1.0
            place[i * N_HEADS + hd, D_ATTN + base + 3 + i] = -1.0
            cq[0, base + 3 + i] = 1.0
            ck[0, base + i] = 1.0
    return jnp.asarray(place, BF16), jnp.asarray(cq), jnp.asarray(ck)


def _layer(x, p, final_norm, plan):
    bsz, seq, _ = x.shape
    tm, tq = plan["tm"], plan["tq"]
    chunks = plan["ffn_chunks"]
    cparams = functools.partial(pltpu.CompilerParams, vmem_limit_bytes=plan["vmem_limit"])
    row2 = lambda v: v.reshape(1, -1)

    w_in = jnp.pad(p["w_in"], ((0, 0), (0, F_PAD - N_HEADS))).astype(BF16)
    fb = jnp.pad(p["forget_b"], (0, F_PAD - N_HEADS)).reshape(1, F_PAD)
    place, cq, ck = _placement()

    tok = lambda width: pl.BlockSpec((None, tm, width), lambda b, s: (b, s, 0))
    n_in = N_IN_MAIN + F_PAD
    tiles_per_seq = seq // tm
    n_tiles = bsz * tiles_per_seq
    cur = lambda n: jnp.minimum(n, n_tiles - 1)
    flat = lambda width: pl.BlockSpec((tm, width), lambda n: (cur(n), 0))
    flat_t = pl.BlockSpec((None, D_ATTN, tm),
                          lambda n: (cur(n) // tiles_per_seq, 0, cur(n) % tiles_per_seq))
    lagged = pl.BlockSpec((tm, D_CONV), lambda n: (jnp.maximum(n - 1, 0), 0))
    attn_t = jax.ShapeDtypeStruct((bsz, D_ATTN, seq), BF16)
    attn_n = jax.ShapeDtypeStruct((bsz * seq, D_ATTN), BF16)
    x1, yc, qt, k, vt, eqt, ek = pl.pallas_call(
        functools.partial(_ffn_in_body, chunks=plan["ffn_in_chunks"], tm=tm,
                          tiles_per_seq=tiles_per_seq),
        grid=(n_tiles + 1,),
        in_specs=[flat(D_MODEL), _resident((1, D_MODEL)), _resident((D_MODEL, 2 * D_FF)),
                  _resident((D_FF, D_MODEL)), _resident((1, D_MODEL)), _resident((D_MODEL, n_in)),
                  _resident((1, F_PAD)), _resident((F_PAD, 2 * D_ATTN)),
                  _resident((1, D_ATTN)), _resident((1, D_ATTN)),
                  _resident((CONV_WIDTH, V7X_SUBLANES, D_CONV))] + [_resident((1, D_CONV))] * 4,
        out_specs=[flat(D_MODEL), lagged, flat_t, flat(D_ATTN), flat_t, flat_t, flat(D_ATTN)],
        out_shape=[jax.ShapeDtypeStruct((bsz * seq, D_MODEL), F32),
                   jax.ShapeDtypeStruct((bsz * seq, D_CONV), BF16),
                   attn_t, attn_n, attn_t, attn_t, attn_n],
        scratch_shapes=[pltpu.VMEM((V7X_SUBLANES, F_PAD), F32),
                        pltpu.VMEM((HALO + tm, D_CONV), F32),
                        pltpu.VMEM((V7X_SUBLANES - 1, HALO + tm - V7X_SUBLANES, D_CONV), F32),
                        pltpu.VMEM((tm, D_CONV), F32)],
        compiler_params=cparams(dimension_semantics=("arbitrary",)),
        name="ffn_in",
    )(x.reshape(bsz * seq, D_MODEL), row2(p["ffn1_norm"]), p["ffn1_w13"].astype(BF16),
      p["ffn1_w2"].astype(BF16), row2(p["mix_norm"]), w_in, fb, place, cq, ck,
      jnp.broadcast_to(p["conv_w"][:, None, :], (CONV_WIDTH, V7X_SUBLANES, D_CONV)),
      row2(p["conv_b"]), row2(p["conv_ln_g"]), row2(p["conv_ln_b"]), row2(p["out_norm_conv"]))
    x1 = x1.reshape(bsz, seq, D_MODEL)
    yc = yc.reshape(bsz, seq, D_CONV)
    k = k.reshape(bsz, seq, D_ATTN)
    ek = ek.reshape(bsz, seq, D_ATTN)

    pair = pl.BlockSpec((None, seq, PAIR), lambda b, g: (b, 0, g))
    pair_t = pl.BlockSpec((None, PAIR, seq), lambda b, g: (b, g, 0))
    steps = bsz * N_PAIRS

    def n_slabs(w):
        return max(d for d in range(1, steps + 1)
                   if steps % d == 0 and w.shape[0] % (d * BF16_SUBLANES) == 0)

    slab = lambda w: w.reshape(n_slabs(w), w.shape[0] // n_slabs(w), w.shape[1])
    slab_spec = lambda w: pl.BlockSpec(
        (None,) + slab(w).shape[1:],
        lambda b, g, per=steps // n_slabs(w): ((b * N_PAIRS + g) // per, 0, 0))
    late_w = [p["w_out"], p["ffn2_w13"], p["ffn2_w2"]]
    ya, w_out, w13_2, w2_2 = pl.pallas_call(
        functools.partial(_attn_body, seq=seq, tq=tq),
        grid=(bsz, N_PAIRS),
        in_specs=[pair_t, pair_t, pair, pair, pair_t] + [slab_spec(w) for w in late_w],
        out_specs=[pair] + [slab_spec(w) for w in late_w],
        out_shape=[jax.ShapeDtypeStruct((bsz, seq, D_ATTN), F32)]
                  + [jax.ShapeDtypeStruct(slab(w).shape, BF16) for w in late_w],
        scratch_shapes=[pltpu.VMEM((seq, 2 * PAIR), BF16),
                        pltpu.VMEM((2, HEAD_DIM + ONES_ROWS, seq), BF16)],
        compiler_params=cparams(dimension_semantics=("arbitrary", "arbitrary")),
        name="attn",
    )(qt, eqt, k, ek, vt, *[slab(w) for w in late_w])
    w_out, w13_2, w2_2 = [w.reshape(src.shape) for w, src in zip((w_out, w13_2, w2_2), late_w)]

    tmo = plan["tm_out"]
    tok = lambda width: pl.BlockSpec((None, tmo, width), lambda b, s: (b, s, 0))
    out = pl.pallas_call(
        functools.partial(_out_ffn_body, chunks=chunks),
        grid=(bsz, seq // tmo),
        in_specs=[tok(D_MODEL), tok(D_CONV), tok(D_ATTN), _resident((1, D_ATTN)),
                  _resident((D_MIX, D_MODEL)), _resident((1, D_MODEL)),
                  _resident((D_MODEL, 2 * D_FF)), _resident((D_FF, D_MODEL)),
                  _resident((1, D_MODEL))],
        out_specs=tok(D_MODEL),
        out_shape=jax.ShapeDtypeStruct((bsz, seq, D_MODEL), F32),
        compiler_params=cparams(dimension_semantics=("arbitrary", "arbitrary")),
        name="out_ffn",
    )(x1, yc, ya, row2(p["out_norm_attn"]), w_out, row2(p["ffn2_norm"]), w13_2, w2_2,
      row2(final_norm))
    return out


def kernel(x, ffn1_norm, ffn1_w13, ffn1_w2, mix_norm, w_in, conv_w, conv_b, conv_ln_g, conv_ln_b,
           forget_b, out_norm_conv, out_norm_attn, w_out, ffn2_norm, ffn2_w13, ffn2_w2, final_norm):
    depth = ffn1_norm.shape[0]
    assert depth == 1, "the fused final RMSNorm assumes a single layer"
    plan = _plan()
    stacked = dict(ffn1_norm=ffn1_norm, ffn1_w13=ffn1_w13, ffn1_w2=ffn1_w2, mix_norm=mix_norm,
                   w_in=w_in, conv_w=conv_w, conv_b=conv_b, conv_ln_g=conv_ln_g,
                   conv_ln_b=conv_ln_b, forget_b=forget_b, out_norm_conv=out_norm_conv,
                   out_norm_attn=out_norm_attn, w_out=w_out, ffn2_norm=ffn2_norm,
                   ffn2_w13=ffn2_w13, ffn2_w2=ffn2_w2)
    layer = {name: arr[0] for name, arr in stacked.items()}
    return _layer(x, layer, final_norm, plan)
```

```python
import functools

import numpy as np
import jax
import jax.numpy as jnp
from jax import lax
from jax.experimental import pallas as pl
from jax.experimental.pallas import tpu as pltpu

D_MODEL = 1024
D_CONV = 512
CONV_WIDTH = 31
N_HEADS = 8
HEAD_DIM = 64
D_ATTN = N_HEADS * HEAD_DIM
D_MIX = D_CONV + D_ATTN
D_FF = 2816
EPS = 1e-6
LOG2E = 1.4426950408889634

V7X_LANES = 128
V7X_SUBLANES = 8
BF16_SUBLANES = 16
V7X_MXU_DIM = 256
V7X_VMEM_BYTES = 64 * 1024 * 1024

PAIR = 2 * HEAD_DIM
N_PAIRS = N_HEADS // 2
F_PAD = V7X_LANES
N_IN_MAIN = 2 * D_CONV + 3 * D_ATTN
HALO = 32
CONV_ROWS = 32
KEY_BLOCKS = 2
ONES_ROWS = 16

BF16 = jnp.bfloat16
F32 = jnp.float32


def _plan():
    def hidden_chunks(mxu_tiles):
        step = mxu_tiles * V7X_MXU_DIM
        return tuple((c, min(c + step, D_FF)) for c in range(0, D_FF, step))

    return dict(
        tm=512,
        tm_out=1024,
        tq=256,
        ffn_chunks=hidden_chunks(6),
        ffn_in_chunks=hidden_chunks(1),
        vmem_limit=V7X_VMEM_BYTES - 8 * 1024 * 1024,
    )


def _rms(x, g):
    return x * lax.rsqrt(jnp.mean(x * x, axis=-1, keepdims=True) + EPS) * g


def _split3(x):
    hi = x.astype(BF16)
    r = x - hi.astype(F32)
    mid = r.astype(BF16)
    lo = (r - mid.astype(F32)).astype(BF16)
    return hi, mid, lo


def _pack3(x):
    hi, mid, lo = _split3(x)
    packed = (hi.astype(F32) + pltpu.roll(mid.astype(F32), N_HEADS, axis=1)
              + pltpu.roll(lo.astype(F32), 2 * N_HEADS, axis=1))
    return packed.astype(BF16)


def _swiglu(h_bf16, w13_ref, w2_ref, chunks, side_work=None):
    acc = None
    zero = None
    for c, (c0, c1) in enumerate(chunks):
        gate = jnp.dot(h_bf16, w13_ref[:, c0:c1], preferred_element_type=F32)
        up = jnp.dot(h_bf16, w13_ref[:, D_FF + c0:D_FF + c1], preferred_element_type=F32)
        if zero is not None:
            gate = gate + zero
        zero = side_work[c]() if side_work is not None else None
        act = (gate * jax.nn.sigmoid(gate) * up).astype(BF16)
        part = jnp.dot(act, w2_ref[c0:c1, :], preferred_element_type=F32)
        acc = part if acc is None else acc + part
    return acc


def _zero_after(v):
    bits = pltpu.bitcast(v[0:V7X_SUBLANES, 0:V7X_LANES], jnp.uint32)
    bits = lax.shift_right_logical(lax.shift_right_logical(bits, jnp.uint32(16)), jnp.uint32(16))
    return pltpu.bitcast(bits, F32)[0:1, 0:1]


def _conv_rows(u_ref, s_ref, cw_ref, base):
    first = HALO - (CONV_WIDTH - 1)
    groups = CONV_ROWS // V7X_SUBLANES
    acc = None
    for kk in range(CONV_WIDTH):
        aligned, r = divmod(first + kk, V7X_SUBLANES)
        src = u_ref if r == 0 else s_ref.at[r - 1]
        lo = base + aligned * V7X_SUBLANES
        term = src[lo:lo + CONV_ROWS, :].reshape(groups, V7X_SUBLANES, D_CONV) * cw_ref[kk]
        acc = term if acc is None else acc + term
    return acc.reshape(CONV_ROWS, D_CONV)


def _ffn_in_body(x_ref, n1_ref, w13_ref, w2_ref, nm_ref, win_ref, fb_ref, place_ref,
                 cq_ref, ck_ref, cw_ref, cb_ref, lg_ref, lb_ref, on_ref,
                 x1_ref, yc_ref, qt_ref, k_ref, vt_ref, eqt_ref, ek_ref,
                 carry_ref, u_ref, s_ref, cacc_ref, *, chunks, tm, tiles_per_seq):
    n = pl.program_id(0)

    @pl.when(n == 0)
    def _():
        u_ref[...] = jnp.zeros_like(u_ref)
        carry_ref[...] = jnp.zeros_like(carry_ref)

    def tile_sum(v):
        v = v.reshape(-1, V7X_SUBLANES, v.shape[-1]).sum(axis=0)
        return sum(v[:, g:g + V7X_LANES] for g in range(0, v.shape[-1], V7X_LANES))

    def shift_copies():
        for r in range(1, V7X_SUBLANES):
            s_ref[r - 1] = u_ref[r:r + s_ref.shape[1], :]
        return None

    def conv_group(bases):
        def run():
            seen = None
            for base in bases:
                rows = _conv_rows(u_ref, s_ref, cw_ref, base)
                cacc_ref[base:base + CONV_ROWS, :] = rows
                seen = tile_sum(rows) if seen is None else seen + tile_sum(rows)
            return _zero_after(seen)
        return run

    def conv_epilogue():
        cv = cacc_ref[...] + cb_ref[...]
        mu = jnp.mean(cv, axis=-1, keepdims=True)
        xc = cv - mu
        y = xc * lax.rsqrt(jnp.mean(xc * xc, axis=-1, keepdims=True) + EPS) * lg_ref[...] + lb_ref[...]
        y = y * jax.nn.sigmoid(y)
        y = _rms(y, on_ref[...])
        yc_ref[...] = y.astype(BF16)
        return _zero_after(tile_sum(y))

    bases = list(range(0, tm, CONV_ROWS))
    n_groups = len(chunks) - 3
    per_group = -(-len(bases) // n_groups)
    side_work = ([shift_copies]
                 + [conv_group(bases[g * per_group:(g + 1) * per_group]) for g in range(n_groups)]
                 + [conv_epilogue, lambda: None])

    flush_step = pl.num_programs(0) - 1

    @pl.when(n == flush_step)
    def _():
        for work in side_work:
            work()

    @pl.when(n < flush_step)
    def _():
        x = x_ref[...]
        h = _rms(x, n1_ref[...]).astype(BF16)
        x1 = x + 0.5 * _swiglu(h, w13_ref, w2_ref, chunks, side_work)
        x1_ref[...] = x1

        h2 = _rms(x1, nm_ref[...]).astype(BF16)
        proj = jnp.dot(h2, win_ref[...], preferred_element_type=F32)

        opens_sequence = lax.rem(n, tiles_per_seq) == 0
        u_ref[0:HALO, :] = jnp.where(opens_sequence, 0.0, u_ref[tm:tm + HALO, :])
        u_ref[HALO:HALO + tm, :] = proj[:, :D_CONV] * jax.nn.sigmoid(proj[:, D_CONV:2 * D_CONV])
        o = 2 * D_CONV
        qt_ref[...] = (proj[:, o:o + D_ATTN] * (HEAD_DIM ** -0.5 * LOG2E)).T.astype(BF16)
        k_ref[...] = proj[:, o + D_ATTN:o + 2 * D_ATTN].astype(BF16)
        vt_ref[...] = proj[:, o + 2 * D_ATTN:o + 3 * D_ATTN].T.astype(BF16)

        z = proj[:, N_IN_MAIN:] + fb_ref[...]
        logf = jnp.minimum(z, 0.0) - jnp.log1p(jnp.exp(-jnp.abs(z)))
        lane = lax.broadcasted_iota(jnp.int32, logf.shape, 1)
        logf = jnp.where(lane < N_HEADS, logf, 0.0)

        row = lax.broadcasted_iota(jnp.int32, (tm, tm), 0)
        col = lax.broadcasted_iota(jnp.int32, (tm, tm), 1)
        tril = jnp.where(row >= col, 1.0, 0.0).astype(BF16)
        d_packed = jnp.dot(tril, _pack3(logf), preferred_element_type=F32)
        d_loc = (d_packed + pltpu.roll(d_packed, F_PAD - N_HEADS, axis=1)
                 + pltpu.roll(d_packed, F_PAD - 2 * N_HEADS, axis=1))
        d_loc = jnp.where(lane < N_HEADS, d_loc, 0.0)

        d = d_loc + jnp.where(opens_sequence, 0.0, carry_ref[0:1, :])
        carry_ref[0:1, :] = d[tm - 1:tm, :]

        placed = jnp.dot(_pack3(d * LOG2E), place_ref[...], preferred_element_type=F32)
        eqt_ref[...] = (placed[:, :D_ATTN] + cq_ref[...]).T.astype(BF16)
        ek_ref[...] = (placed[:, D_ATTN:] + ck_ref[...]).astype(BF16)


def _attn_body(qt_ref, eqt_ref, k_ref, ek_ref, vt_ref, wa_ref, wb_ref, wc_ref,
               o_ref, wa_out, wb_out, wc_out, ka_ref, va_ref, *, seq, tq):
    for w_in_ref, w_out_ref in ((wa_ref, wa_out), (wb_ref, wb_out), (wc_ref, wc_out)):
        w_out_ref[...] = w_in_ref[...].astype(BF16)

    ka_ref[:, 0:PAIR] = k_ref[...]
    ka_ref[:, PAIR:2 * PAIR] = ek_ref[...]
    for hh in range(2):
        va_ref[hh, 0:HEAD_DIM, :] = vt_ref[hh * HEAD_DIM:(hh + 1) * HEAD_DIM, :]
        va_ref[hh, HEAD_DIM:HEAD_DIM + ONES_ROWS, :] = jnp.ones((ONES_ROWS, seq), BF16)

    key = lax.broadcasted_iota(jnp.int32, (tq, 2 * tq), 0)
    qry = lax.broadcasted_iota(jnp.int32, (tq, 2 * tq), 1)
    causal = key <= jnp.where(qry < tq, qry, qry - tq)
    zeros = jnp.zeros((HEAD_DIM, tq), BF16)

    def masked_queries(i):
        q, e = qt_ref[:, i * tq:(i + 1) * tq], eqt_ref[:, i * tq:(i + 1) * tq]
        h0 = jnp.concatenate([q[0:HEAD_DIM], zeros, e[0:HEAD_DIM], zeros], axis=0)
        h1 = jnp.concatenate([zeros, q[HEAD_DIM:PAIR], zeros, e[HEAD_DIM:PAIR]], axis=0)
        return jnp.concatenate([h0, h1], axis=1)

    qm = {}

    def scores(i, j, w):
        if i not in qm:
            qm.clear()
            qm[i] = masked_queries(i)
        s = jnp.dot(ka_ref[j * tq:(j + w) * tq, :], qm[i], preferred_element_type=F32)
        if j + w <= i:
            return s
        diag = jnp.where(causal, s[(w - 1) * tq:, :], -jnp.inf)
        return diag if w == 1 else jnp.concatenate([s[0:(w - 1) * tq, :], diag], axis=0)

    tasks = [(i, j, min(KEY_BLOCKS, i + 1 - j))
             for i in range(seq // tq) for j in range(0, i + 1, KEY_BLOCKS)]
    s_next = scores(*tasks[0])
    m = acc = None
    for n, (i, j, w) in enumerate(tasks):
        s = s_next
        if n + 1 < len(tasks):
            s_next = scores(*tasks[n + 1])
        m_blk = jnp.max(s, axis=0, keepdims=True)
        m_new = m_blk if j == 0 else jnp.maximum(m, m_blk)
        p = jnp.exp2(s - m_new).astype(BF16)
        o = [jnp.dot(va_ref[hh, :, j * tq:(j + w) * tq], p[:, hh * tq:(hh + 1) * tq],
                     preferred_element_type=F32) for hh in range(2)]
        if j == 0:
            acc = o
        else:
            alpha = jnp.exp2(m - m_new)
            acc = [acc[hh] * alpha[:, hh * tq:(hh + 1) * tq] + o[hh] for hh in range(2)]
        m = m_new
        if j + w == i + 1:
            heads = [a[0:HEAD_DIM, :] * (1.0 / a[HEAD_DIM:HEAD_DIM + 1, :]) for a in acc]
            o_ref[i * tq:(i + 1) * tq, :] = jnp.concatenate(heads, axis=0).T


def _out_ffn_body(x1_ref, yc_ref, ya_ref, na_ref, wo_ref, n2_ref, w13_ref, w2_ref, nf_ref,
                  o_ref, *, chunks):
    ya = _rms(ya_ref[...], na_ref[...]).astype(BF16)
    y = (jnp.dot(yc_ref[...], wo_ref[0:D_CONV, :], preferred_element_type=F32)
         + jnp.dot(ya, wo_ref[D_CONV:D_MIX, :], preferred_element_type=F32))
    x2 = x1_ref[...] + y
    h = _rms(x2, n2_ref[...]).astype(BF16)
    x3 = x2 + 0.5 * _swiglu(h, w13_ref, w2_ref, chunks)
    o_ref[...] = _rms(x3, nf_ref[...])


def _resident(shape):
    nd = len(shape)
    return pl.BlockSpec(shape, lambda *_: (0,) * nd, pipeline_mode=pl.Buffered(1))


def _placement():
    place = np.zeros((F_PAD, 2 * D_ATTN), np.float32)
    cq = np.zeros((1, D_ATTN), np.float32)
    ck = np.zeros((1, D_ATTN), np.float32)
    for hd in range(N_HEADS):
        base = hd * HEAD_DIM
        for i in range(3):
            place[i * N_HEADS + hd, base + i] = 1.0
            place[i * N_HEADS + hd, D_ATTN + base + 3 + i] = -1.0
            cq[0, base + 3 + i] = 1.0
            ck[0, base + i] = 1.0
    return jnp.asarray(place, BF16), jnp.asarray(cq), jnp.asarray(ck)


def _layer(x, p, final_norm, plan):
    bsz, seq, _ = x.shape
    tm, tq = plan["tm"], plan["tq"]
    chunks = plan["ffn_chunks"]
    cparams = functools.partial(pltpu.CompilerParams, vmem_limit_bytes=plan["vmem_limit"])
    row2 = lambda v: v.reshape(1, -1)

    w_in = jnp.pad(p["w_in"], ((0, 0), (0, F_PAD - N_HEADS))).astype(BF16)
    fb = jnp.pad(p["forget_b"], (0, F_PAD - N_HEADS)).reshape(1, F_PAD)
    place, cq, ck = _placement()

    tok = lambda width: pl.BlockSpec((None, tm, width), lambda b, s: (b, s, 0))
    n_in = N_IN_MAIN + F_PAD
    tiles_per_seq = seq // tm
    n_tiles = bsz * tiles_per_seq
    cur = lambda n: jnp.minimum(n, n_tiles - 1)
    flat = lambda width: pl.BlockSpec((tm, width), lambda n: (cur(n), 0))
    flat_t = pl.BlockSpec((None, D_ATTN, tm),
                          lambda n: (cur(n) // tiles_per_seq, 0, cur(n) % tiles_per_seq))
    lagged = pl.BlockSpec((tm, D_CONV), lambda n: (jnp.maximum(n - 1, 0), 0))
    attn_t = jax.ShapeDtypeStruct((bsz, D_ATTN, seq), BF16)
    attn_n = jax.ShapeDtypeStruct((bsz * seq, D_ATTN), BF16)
    x1, yc, qt, k, vt, eqt, ek = pl.pallas_call(
        functools.partial(_ffn_in_body, chunks=plan["ffn_in_chunks"], tm=tm,
                          tiles_per_seq=tiles_per_seq),
        grid=(n_tiles + 1,),
        in_specs=[flat(D_MODEL), _resident((1, D_MODEL)), _resident((D_MODEL, 2 * D_FF)),
                  _resident((D_FF, D_MODEL)), _resident((1, D_MODEL)), _resident((D_MODEL, n_in)),
                  _resident((1, F_PAD)), _resident((F_PAD, 2 * D_ATTN)),
                  _resident((1, D_ATTN)), _resident((1, D_ATTN)),
                  _resident((CONV_WIDTH, V7X_SUBLANES, D_CONV))] + [_resident((1, D_CONV))] * 4,
        out_specs=[flat(D_MODEL), lagged, flat_t, flat(D_ATTN), flat_t, flat_t, flat(D_ATTN)],
        out_shape=[jax.ShapeDtypeStruct((bsz * seq, D_MODEL), F32),
                   jax.ShapeDtypeStruct((bsz * seq, D_CONV), BF16),
                   attn_t, attn_n, attn_t, attn_t, attn_n],
        scratch_shapes=[pltpu.VMEM((V7X_SUBLANES, F_PAD), F32),
                        pltpu.VMEM((HALO + tm, D_CONV), F32),
                        pltpu.VMEM((V7X_SUBLANES - 1, HALO + tm - V7X_SUBLANES, D_CONV), F32),
                        pltpu.VMEM((tm, D_CONV), F32)],
        compiler_params=cparams(dimension_semantics=("arbitrary",)),
        name="ffn_in",
    )(x.reshape(bsz * seq, D_MODEL), row2(p["ffn1_norm"]), p["ffn1_w13"].astype(BF16),
      p["ffn1_w2"].astype(BF16), row2(p["mix_norm"]), w_in, fb, place, cq, ck,
      jnp.broadcast_to(p["conv_w"][:, None, :], (CONV_WIDTH, V7X_SUBLANES, D_CONV)),
      row2(p["conv_b"]), row2(p["conv_ln_g"]), row2(p["conv_ln_b"]), row2(p["out_norm_conv"]))
    x1 = x1.reshape(bsz, seq, D_MODEL)
    yc = yc.reshape(bsz, seq, D_CONV)
    k = k.reshape(bsz, seq, D_ATTN)
    ek = ek.reshape(bsz, seq, D_ATTN)

    pair = pl.BlockSpec((None, seq, PAIR), lambda b, g: (b, 0, g))
    pair_t = pl.BlockSpec((None, PAIR, seq), lambda b, g: (b, g, 0))
    steps = bsz * N_PAIRS

    def n_slabs(w):
        return max(d for d in range(1, steps + 1)
                   if steps % d == 0 and w.shape[0] % (d * BF16_SUBLANES) == 0)

    slab = lambda w: w.reshape(n_slabs(w), w.shape[0] // n_slabs(w), w.shape[1])
    slab_spec = lambda w: pl.BlockSpec(
        (None,) + slab(w).shape[1:],
        lambda b, g, per=steps // n_slabs(w): ((b * N_PAIRS + g) // per, 0, 0))
    late_w = [p["w_out"], p["ffn2_w13"], p["ffn2_w2"]]
    ya, w_out, w13_2, w2_2 = pl.pallas_call(
        functools.partial(_attn_body, seq=seq, tq=tq),
        grid=(bsz, N_PAIRS),
        in_specs=[pair_t, pair_t, pair, pair, pair_t] + [slab_spec(w) for w in late_w],
        out_specs=[pair] + [slab_spec(w) for w in late_w],
        out_shape=[jax.ShapeDtypeStruct((bsz, seq, D_ATTN), F32)]
                  + [jax.ShapeDtypeStruct(slab(w).shape, BF16) for w in late_w],
        scratch_shapes=[pltpu.VMEM((seq, 2 * PAIR), BF16),
                        pltpu.VMEM((2, HEAD_DIM + ONES_ROWS, seq), BF16)],
        compiler_params=cparams(dimension_semantics=("arbitrary", "arbitrary")),
        name="attn",
    )(qt, eqt, k, ek, vt, *[slab(w) for w in late_w])
    w_out, w13_2, w2_2 = [w.reshape(src.shape) for w, src in zip((w_out, w13_2, w2_2), late_w)]

    tmo = plan["tm_out"]
    tok = lambda width: pl.BlockSpec((None, tmo, width), lambda b, s: (b, s, 0))
    out = pl.pallas_call(
        functools.partial(_out_ffn_body, chunks=chunks),
        grid=(bsz, seq // tmo),
        in_specs=[tok(D_MODEL), tok(D_CONV), tok(D_ATTN), _resident((1, D_ATTN)),
                  _resident((D_MIX, D_MODEL)), _resident((1, D_MODEL)),
                  _resident((D_MODEL, 2 * D_FF)), _resident((D_FF, D_MODEL)),
                  _resident((1, D_MODEL))],
        out_specs=tok(D_MODEL),
        out_shape=jax.ShapeDtypeStruct((bsz, seq, D_MODEL), F32),
        compiler_params=cparams(dimension_semantics=("arbitrary", "arbitrary")),
        name="out_ffn",
    )(x1, yc, ya, row2(p["out_norm_attn"]), w_out, row2(p["ffn2_norm"]), w13_2, w2_2,
      row2(final_norm))
    return out


def kernel(x, ffn1_norm, ffn1_w13, ffn1_w2, mix_norm, w_in, conv_w, conv_b, conv_ln_g, conv_ln_b,
           forget_b, out_norm_conv, out_norm_attn, w_out, ffn2_norm, ffn2_w13, ffn2_w2, final_norm):
    depth = ffn1_norm.shape[0]
    assert depth == 1, "the fused final RMSNorm assumes a single layer"
    plan = _plan()
    stacked = dict(ffn1_norm=ffn1_norm, ffn1_w13=ffn1_w13, ffn1_w2=ffn1_w2, mix_norm=mix_norm,
                   w_in=w_in, conv_w=conv_w, conv_b=conv_b, conv_ln_g=conv_ln_g,
                   conv_ln_b=conv_ln_b, forget_b=forget_b, out_norm_conv=out_norm_conv,
                   out_norm_attn=out_norm_attn, w_out=w_out, ffn2_norm=ffn2_norm,
                   ffn2_w13=ffn2_w13, ffn2_w2=ffn2_w2)
    layer = {name: arr[0] for name, arr in stacked.items()}
    return _layer(x, layer, final_norm, plan)
```

```python
import functools

import numpy as np
import jax
import jax.numpy as jnp
from jax import lax
from jax.experimental import pallas as pl
from jax.experimental.pallas import tpu as pltpu

D_MODEL = 1024
D_CONV = 512
CONV_WIDTH = 31
N_HEADS = 8
HEAD_DIM = 64
D_ATTN = N_HEADS * HEAD_DIM
D_MIX = D_CONV + D_ATTN
D_FF = 2816
EPS = 1e-6
LOG2E = 1.4426950408889634

V7X_LANES = 128
V7X_SUBLANES = 8
BF16_SUBLANES = 16
V7X_MXU_DIM = 256
V7X_VMEM_BYTES = 64 * 1024 * 1024

PAIR = 2 * HEAD_DIM
N_PAIRS = N_HEADS // 2
F_PAD = V7X_LANES
N_IN_MAIN = 2 * D_CONV + 3 * D_ATTN
HALO = 32
CONV_ROWS = 16
KEY_BLOCKS = 2
ONES_ROWS = 16

BF16 = jnp.bfloat16
F32 = jnp.float32


def _plan():
    def hidden_chunks(mxu_tiles):
        step = mxu_tiles * V7X_MXU_DIM
        return tuple((c, min(c + step, D_FF)) for c in range(0, D_FF, step))

    return dict(
        tm=512,
        tm_out=1024,
        tq=256,
        ffn_chunks=hidden_chunks(6),
        ffn_in_chunks=hidden_chunks(1),
        vmem_limit=V7X_VMEM_BYTES - 8 * 1024 * 1024,
    )


def _rms(x, g):
    return x * lax.rsqrt(jnp.mean(x * x, axis=-1, keepdims=True) + EPS) * g


def _split3(x):
    hi = x.astype(BF16)
    r = x - hi.astype(F32)
    mid = r.astype(BF16)
    lo = (r - mid.astype(F32)).astype(BF16)
    return hi, mid, lo


def _pack3(x):
    hi, mid, lo = _split3(x)
    packed = (hi.astype(F32) + pltpu.roll(mid.astype(F32), N_HEADS, axis=1)
              + pltpu.roll(lo.astype(F32), 2 * N_HEADS, axis=1))
    return packed.astype(BF16)


def _swiglu(h_bf16, w13_ref, w2_ref, chunks, side_work=None):
    acc = None
    zero = None
    for c, (c0, c1) in enumerate(chunks):
        gate = jnp.dot(h_bf16, w13_ref[:, c0:c1], preferred_element_type=F32)
        up = jnp.dot(h_bf16, w13_ref[:, D_FF + c0:D_FF + c1], preferred_element_type=F32)
        if zero is not None:
            gate = gate + zero
        zero = side_work[c]() if side_work is not None else None
        act = (gate * jax.nn.sigmoid(gate) * up).astype(BF16)
        part = jnp.dot(act, w2_ref[c0:c1, :], preferred_element_type=F32)
        acc = part if acc is None else acc + part
    return acc


def _zero_after(v):
    bits = pltpu.bitcast(v[0:V7X_SUBLANES, 0:V7X_LANES], jnp.uint32)
    bits = lax.shift_right_logical(lax.shift_right_logical(bits, jnp.uint32(16)), jnp.uint32(16))
    return pltpu.bitcast(bits, F32)[0:1, 0:1]


def _conv_rows(u_ref, s_ref, cw_ref, base):
    first = HALO - (CONV_WIDTH - 1)
    groups = CONV_ROWS // V7X_SUBLANES
    acc = None
    for kk in range(CONV_WIDTH):
        aligned, r = divmod(first + kk, V7X_SUBLANES)
        src = u_ref if r == 0 else s_ref.at[r - 1]
        lo = base + aligned * V7X_SUBLANES
        term = src[lo:lo + CONV_ROWS, :].reshape(groups, V7X_SUBLANES, D_CONV) * cw_ref[kk]
        acc = term if acc is None else acc + term
    return acc.reshape(CONV_ROWS, D_CONV)


def _ffn_in_body(x_ref, n1_ref, w13_ref, w2_ref, nm_ref, win_ref, fb_ref, place_ref,
                 cq_ref, ck_ref, cw_ref, cb_ref, lg_ref, lb_ref, on_ref,
                 x1_ref, yc_ref, qt_ref, k_ref, vt_ref, eqt_ref, ek_ref,
                 carry_ref, u_ref, s_ref, cacc_ref, *, chunks, tm, tiles_per_seq):
    n = pl.program_id(0)

    @pl.when(n == 0)
    def _():
        u_ref[...] = jnp.zeros_like(u_ref)
        carry_ref[...] = jnp.zeros_like(carry_ref)

    def tile_sum(v):
        v = v.reshape(-1, V7X_SUBLANES, v.shape[-1]).sum(axis=0)
        return sum(v[:, g:g + V7X_LANES] for g in range(0, v.shape[-1], V7X_LANES))

    def shift_copies():
        for r in range(1, V7X_SUBLANES):
            s_ref[r - 1] = u_ref[r:r + s_ref.shape[1], :]
        return None

    def conv_group(bases):
        def run():
            seen = None
            for base in bases:
                rows = _conv_rows(u_ref, s_ref, cw_ref, base)
                cacc_ref[base:base + CONV_ROWS, :] = rows
                seen = tile_sum(rows) if seen is None else seen + tile_sum(rows)
            return _zero_after(seen)
        return run

    def conv_epilogue():
        cv = cacc_ref[...] + cb_ref[...]
        mu = jnp.mean(cv, axis=-1, keepdims=True)
        xc = cv - mu
        y = xc * lax.rsqrt(jnp.mean(xc * xc, axis=-1, keepdims=True) + EPS) * lg_ref[...] + lb_ref[...]
        y = y * jax.nn.sigmoid(y)
        y = _rms(y, on_ref[...])
        yc_ref[...] = y.astype(BF16)
        return _zero_after(tile_sum(y))

    bases = list(range(0, tm, CONV_ROWS))
    n_groups = len(chunks) - 3
    per_group = -(-len(bases) // n_groups)
    side_work = ([shift_copies]
                 + [conv_group(bases[g * per_group:(g + 1) * per_group]) for g in range(n_groups)]
                 + [conv_epilogue, lambda: None])

    flush_step = pl.num_programs(0) - 1

    @pl.when(n == flush_step)
    def _():
        for work in side_work:
            work()

    @pl.when(n < flush_step)
    def _():
        x = x_ref[...]
        h = _rms(x, n1_ref[...]).astype(BF16)
        x1 = x + 0.5 * _swiglu(h, w13_ref, w2_ref, chunks, side_work)
        x1_ref[...] = x1

        h2 = _rms(x1, nm_ref[...]).astype(BF16)
        proj = jnp.dot(h2, win_ref[...], preferred_element_type=F32)

        opens_sequence = lax.rem(n, tiles_per_seq) == 0
        u_ref[0:HALO, :] = jnp.where(opens_sequence, 0.0, u_ref[tm:tm + HALO, :])
        u_ref[HALO:HALO + tm, :] = proj[:, :D_CONV] * jax.nn.sigmoid(proj[:, D_CONV:2 * D_CONV])
        o = 2 * D_CONV
        qt_ref[...] = (proj[:, o:o + D_ATTN] * (HEAD_DIM ** -0.5 * LOG2E)).T.astype(BF16)
        k_ref[...] = proj[:, o + D_ATTN:o + 2 * D_ATTN].astype(BF16)
        vt_ref[...] = proj[:, o + 2 * D_ATTN:o + 3 * D_ATTN].T.astype(BF16)

        z = proj[:, N_IN_MAIN:] + fb_ref[...]
        logf = jnp.minimum(z, 0.0) - jnp.log1p(jnp.exp(-jnp.abs(z)))
        lane = lax.broadcasted_iota(jnp.int32, logf.shape, 1)
        logf = jnp.where(lane < N_HEADS, logf, 0.0)

        row = lax.broadcasted_iota(jnp.int32, (tm, tm), 0)
        col = lax.broadcasted_iota(jnp.int32, (tm, tm), 1)
        tril = jnp.where(row >= col, 1.0, 0.0).astype(BF16)
        d_packed = jnp.dot(tril, _pack3(logf), preferred_element_type=F32)
        d_loc = (d_packed + pltpu.roll(d_packed, F_PAD - N_HEADS, axis=1)
                 + pltpu.roll(d_packed, F_PAD - 2 * N_HEADS, axis=1))
        d_loc = jnp.where(lane < N_HEADS, d_loc, 0.0)

        d = d_loc + jnp.where(opens_sequence, 0.0, carry_ref[0:1, :])
        carry_ref[0:1, :] = d[tm - 1:tm, :]

        placed = jnp.dot(_pack3(d * LOG2E), place_ref[...], preferred_element_type=F32)
        eqt_ref[...] = (placed[:, :D_ATTN] + cq_ref[...]).T.astype(BF16)
        ek_ref[...] = (placed[:, D_ATTN:] + ck_ref[...]).astype(BF16)


def _attn_body(qt_ref, eqt_ref, k_ref, ek_ref, vt_ref, wa_ref, wb_ref, wc_ref,
               o_ref, wa_out, wb_out, wc_out, ka_ref, va_ref, *, seq, tq):
    for w_in_ref, w_out_ref in ((wa_ref, wa_out), (wb_ref, wb_out), (wc_ref, wc_out)):
        w_out_ref[...] = w_in_ref[...].astype(BF16)

    ka_ref[:, 0:PAIR] = k_ref[...]
    ka_ref[:, PAIR:2 * PAIR] = ek_ref[...]
    for hh in range(2):
        va_ref[hh, 0:HEAD_DIM, :] = vt_ref[hh * HEAD_DIM:(hh + 1) * HEAD_DIM, :]
        va_ref[hh, HEAD_DIM:HEAD_DIM + ONES_ROWS, :] = jnp.ones((ONES_ROWS, seq), BF16)

    key = lax.broadcasted_iota(jnp.int32, (tq, 2 * tq), 0)
    qry = lax.broadcasted_iota(jnp.int32, (tq, 2 * tq), 1)
    causal = key <= jnp.where(qry < tq, qry, qry - tq)
    zeros = jnp.zeros((HEAD_DIM, tq), BF16)

    def masked_queries(i):
        q, e = qt_ref[:, i * tq:(i + 1) * tq], eqt_ref[:, i * tq:(i + 1) * tq]
        h0 = jnp.concatenate([q[0:HEAD_DIM], zeros, e[0:HEAD_DIM], zeros], axis=0)
        h1 = jnp.concatenate([zeros, q[HEAD_DIM:PAIR], zeros, e[HEAD_DIM:PAIR]], axis=0)
        return jnp.concatenate([h0, h1], axis=1)

    qm = {}

    def scores(i, j, w):
        if i not in qm:
            qm.clear()
            qm[i] = masked_queries(i)
        s = jnp.dot(ka_ref[j * tq:(j + w) * tq, :], qm[i], preferred_element_type=F32)
        if j + w <= i:
            return s
        diag = jnp.where(causal, s[(w - 1) * tq:, :], -jnp.inf)
        return diag if w == 1 else jnp.concatenate([s[0:(w - 1) * tq, :], diag], axis=0)

    tasks = [(i, j, min(KEY_BLOCKS, i + 1 - j))
             for i in range(seq // tq) for j in range(0, i + 1, KEY_BLOCKS)]
    s_next = scores(*tasks[0])
    m = acc = None
    for n, (i, j, w) in enumerate(tasks):
        s = s_next
        if n + 1 < len(tasks):
            s_next = scores(*tasks[n + 1])
        m_blk = jnp.max(s, axis=0, keepdims=True)
        m_new = m_blk if j == 0 else jnp.maximum(m, m_blk)
        p = jnp.exp2(s - m_new).astype(BF16)
        o = [jnp.dot(va_ref[hh, :, j * tq:(j + w) * tq], p[:, hh * tq:(hh + 1) * tq],
                     preferred_element_type=F32) for hh in range(2)]
        if j == 0:
            acc = o
        else:
            alpha = jnp.exp2(m - m_new)
            acc = [acc[hh] * alpha[:, hh * tq:(hh + 1) * tq] + o[hh] for hh in range(2)]
        m = m_new
        if j + w == i + 1:
            heads = [a[0:HEAD_DIM, :] * (1.0 / a[HEAD_DIM:HEAD_DIM + 1, :]) for a in acc]
            o_ref[i * tq:(i + 1) * tq, :] = jnp.concatenate(heads, axis=0).T


def _out_ffn_body(x1_ref, yc_ref, ya_ref, na_ref, wo_ref, n2_ref, w13_ref, w2_ref, nf_ref,
                  o_ref, *, chunks):
    ya = _rms(ya_ref[...], na_ref[...]).astype(BF16)
    y = (jnp.dot(yc_ref[...], wo_ref[0:D_CONV, :], preferred_element_type=F32)
         + jnp.dot(ya, wo_ref[D_CONV:D_MIX, :], preferred_element_type=F32))
    x2 = x1_ref[...] + y
    h = _rms(x2, n2_ref[...]).astype(BF16)
    x3 = x2 + 0.5 * _swiglu(h, w13_ref, w2_ref, chunks)
    o_ref[...] = _rms(x3, nf_ref[...])


def _resident(shape):
    nd = len(shape)
    return pl.BlockSpec(shape, lambda *_: (0,) * nd, pipeline_mode=pl.Buffered(1))


def _placement():
    place = np.zeros((F_PAD, 2 * D_ATTN), np.float32)
    cq = np.zeros((1, D_ATTN), np.float32)
    ck = np.zeros((1, D_ATTN), np.float32)
    for hd in range(N_HEADS):
        base = hd * HEAD_DIM
        for i in range(3):
            place[i * N_HEADS + hd, base + i] = 1.0
            place[i * N_HEADS + hd, D_ATTN + base + 3 + i] = -1.0
            cq[0, base + 3 + i] = 1.0
            ck[0, base + i] = 1.0
    return jnp.asarray(place, BF16), jnp.asarray(cq), jnp.asarray(ck)


def _layer(x, p, final_norm, plan):
    bsz, seq, _ = x.shape
    tm, tq = plan["tm"], plan["tq"]
    chunks = plan["ffn_chunks"]
    cparams = functools.partial(pltpu.CompilerParams, vmem_limit_bytes=plan["vmem_limit"])
    row2 = lambda v: v.reshape(1, -1)

    w_in = jnp.pad(p["w_in"].astype(BF16), ((0, 0), (0, F_PAD - N_HEADS)))
    fb = jnp.pad(p["forget_b"], (0, F_PAD - N_HEADS)).reshape(1, F_PAD)
    place, cq, ck = _placement()

    tiles_per_seq = seq // tm
    n_tiles = bsz * tiles_per_seq
    cur = lambda n: jnp.minimum(n, n_tiles - 1)
    flat = lambda width: pl.BlockSpec((tm, width), lambda n: (cur(n), 0))
    flat_t = pl.BlockSpec((None, D_ATTN, tm),
                          lambda n: (cur(n) // tiles_per_seq, 0, cur(n) % tiles_per_seq))
    lagged = pl.BlockSpec((tm, D_CONV), lambda n: (jnp.maximum(n - 1, 0), 0))
    attn_t = jax.ShapeDtypeStruct((bsz, D_ATTN, seq), BF16)
    attn_n = jax.ShapeDtypeStruct((bsz * seq, D_ATTN), BF16)
    x1, yc, qt, k, vt, eqt, ek = pl.pallas_call(
        functools.partial(_ffn_in_body, chunks=plan["ffn_in_chunks"], tm=tm,
                          tiles_per_seq=tiles_per_seq),
        grid=(n_tiles + 1,),
        in_specs=[flat(D_MODEL), _resident((1, D_MODEL)), _resident((D_MODEL, 2 * D_FF)),
                  _resident((D_FF, D_MODEL)), _resident((1, D_MODEL)),
                  _resident((D_MODEL, N_IN_MAIN + F_PAD)),
                  _resident((1, F_PAD)), _resident((F_PAD, 2 * D_ATTN)),
                  _resident((1, D_ATTN)), _resident((1, D_ATTN)),
                  _resident((CONV_WIDTH, V7X_SUBLANES, D_CONV))] + [_resident((1, D_CONV))] * 4,
        out_specs=[flat(D_MODEL), lagged, flat_t, flat(D_ATTN), flat_t, flat_t, flat(D_ATTN)],
        out_shape=[jax.ShapeDtypeStruct((bsz * seq, D_MODEL), F32),
                   jax.ShapeDtypeStruct((bsz * seq, D_CONV), BF16),
                   attn_t, attn_n, attn_t, attn_t, attn_n],
        scratch_shapes=[pltpu.VMEM((V7X_SUBLANES, F_PAD), F32),
                        pltpu.VMEM((HALO + tm, D_CONV), F32),
                        pltpu.VMEM((V7X_SUBLANES - 1, HALO + tm - V7X_SUBLANES, D_CONV), F32),
                        pltpu.VMEM((tm, D_CONV), F32)],
        compiler_params=cparams(dimension_semantics=("arbitrary",)),
        name="ffn_in",
    )(x.reshape(bsz * seq, D_MODEL), row2(p["ffn1_norm"]), p["ffn1_w13"].astype(BF16),
      p["ffn1_w2"].astype(BF16), row2(p["mix_norm"]), w_in, fb, place, cq, ck,
      jnp.broadcast_to(p["conv_w"][:, None, :], (CONV_WIDTH, V7X_SUBLANES, D_CONV)),
      row2(p["conv_b"]), row2(p["conv_ln_g"]), row2(p["conv_ln_b"]), row2(p["out_norm_conv"]))
    x1 = x1.reshape(bsz, seq, D_MODEL)
    yc = yc.reshape(bsz, seq, D_CONV)
    k = k.reshape(bsz, seq, D_ATTN)
    ek = ek.reshape(bsz, seq, D_ATTN)

    pair = pl.BlockSpec((None, seq, PAIR), lambda b, g: (b, 0, g))
    pair_t = pl.BlockSpec((None, PAIR, seq), lambda b, g: (b, g, 0))
    steps = bsz * N_PAIRS

    def n_slabs(w):
        return max(d for d in range(1, steps + 1)
                   if steps % d == 0 and w.shape[0] % (d * BF16_SUBLANES) == 0)

    slab = lambda w: w.reshape(n_slabs(w), w.shape[0] // n_slabs(w), w.shape[1])
    slab_spec = lambda w: pl.BlockSpec(
        (None,) + slab(w).shape[1:],
        lambda b, g, per=steps // n_slabs(w): ((b * N_PAIRS + g) // per, 0, 0))
    late_w = [p["w_out"], p["ffn2_w13"], p["ffn2_w2"]]
    ya, w_out, w13_2, w2_2 = pl.pallas_call(
        functools.partial(_attn_body, seq=seq, tq=tq),
        grid=(bsz, N_PAIRS),
        in_specs=[pair_t, pair_t, pair, pair, pair_t] + [slab_spec(w) for w in late_w],
        out_specs=[pair] + [slab_spec(w) for w in late_w],
        out_shape=[jax.ShapeDtypeStruct((bsz, seq, D_ATTN), F32)]
                  + [jax.ShapeDtypeStruct(slab(w).shape, BF16) for w in late_w],
        scratch_shapes=[pltpu.VMEM((seq, 2 * PAIR), BF16),
                        pltpu.VMEM((2, HEAD_DIM + ONES_ROWS, seq), BF16)],
        compiler_params=cparams(dimension_semantics=("arbitrary", "arbitrary")),
        name="attn",
    )(qt, eqt, k, ek, vt, *[slab(w) for w in late_w])
    w_out, w13_2, w2_2 = [w.reshape(src.shape) for w, src in zip((w_out, w13_2, w2_2), late_w)]

    tmo = plan["tm_out"]
    tok = lambda width: pl.BlockSpec((None, tmo, width), lambda b, s: (b, s, 0))
    out = pl.pallas_call(
        functools.partial(_out_ffn_body, chunks=chunks),
        grid=(bsz, seq // tmo),
        in_specs=[tok(D_MODEL), tok(D_CONV), tok(D_ATTN), _resident((1, D_ATTN)),
                  _resident((D_MIX, D_MODEL)), _resident((1, D_MODEL)),
                  _resident((D_MODEL, 2 * D_FF)), _resident((D_FF, D_MODEL)),
                  _resident((1, D_MODEL))],
        out_specs=tok(D_MODEL),
        out_shape=jax.ShapeDtypeStruct((bsz, seq, D_MODEL), F32),
        compiler_params=cparams(dimension_semantics=("arbitrary", "arbitrary")),
        name="out_ffn",
    )(x1, yc, ya, row2(p["out_norm_attn"]), w_out, row2(p["ffn2_norm"]), w13_2, w2_2,
      row2(final_norm))
    return out


def kernel(x, ffn1_norm, ffn1_w13, ffn1_w2, mix_norm, w_in, conv_w, conv_b, conv_ln_g, conv_ln_b,
           forget_b, out_norm_conv, out_norm_attn, w_out, ffn2_norm, ffn2_w13, ffn2_w2, final_norm):
    depth = ffn1_norm.shape[0]
    assert depth == 1, "the fused final RMSNorm assumes a single layer"
    plan = _plan()
    stacked = dict(ffn1_norm=ffn1_norm, ffn1_w13=ffn1_w13, ffn1_w2=ffn1_w2, mix_norm=mix_norm,
                   w_in=w_in, conv_w=conv_w, conv_b=conv_b, conv_ln_g=conv_ln_g,
                   conv_ln_b=conv_ln_b, forget_b=forget_b, out_norm_conv=out_norm_conv,
                   out_norm_attn=out_norm_attn, w_out=w_out, ffn2_norm=ffn2_norm,
                   ffn2_w13=ffn2_w13, ffn2_w2=ffn2_w2)
    layer = {name: arr[0] for name, arr in stacked.items()}
    return _layer(x, layer, final_norm, plan)
```

```python
import functools

import numpy as np
import jax
import jax.numpy as jnp
from jax import lax
from jax.experimental import pallas as pl
from jax.experimental.pallas import tpu as pltpu

D_MODEL = 1024
D_CONV = 512
CONV_WIDTH = 31
N_HEADS = 8
HEAD_DIM = 64
D_ATTN = N_HEADS * HEAD_DIM
D_MIX = D_CONV + D_ATTN
D_FF = 2816
EPS = 1e-6
LOG2E = 1.4426950408889634

V7X_LANES = 128
V7X_SUBLANES = 8
BF16_SUBLANES = 16
V7X_MXU_DIM = 256
V7X_VMEM_BYTES = 64 * 1024 * 1024

PAIR = 2 * HEAD_DIM
N_PAIRS = N_HEADS // 2
F_PAD = V7X_LANES
N_IN_MAIN = 2 * D_CONV + 3 * D_ATTN
HALO = 32
CONV_ROWS = 16
LOAD_ROWS, LOAD_COLS = 256, 512
KEY_BLOCKS = 2
ONES_ROWS = 16

BF16 = jnp.bfloat16
F32 = jnp.float32


def _plan():
    def hidden_chunks(mxu_tiles):
        step = mxu_tiles * V7X_MXU_DIM
        return tuple((c, min(c + step, D_FF)) for c in range(0, D_FF, step))

    return dict(
        tm=512,
        tm_out=1024,
        tq=256,
        ffn_chunks=hidden_chunks(6),
        ffn_in_chunks=hidden_chunks(1),
        vmem_limit=V7X_VMEM_BYTES - 8 * 1024 * 1024,
    )


def _rms(x, g):
    return x * lax.rsqrt(jnp.mean(x * x, axis=-1, keepdims=True) + EPS) * g


def _split3(x):
    hi = x.astype(BF16)
    r = x - hi.astype(F32)
    mid = r.astype(BF16)
    lo = (r - mid.astype(F32)).astype(BF16)
    return hi, mid, lo


def _pack3(x):
    hi, mid, lo = _split3(x)
    packed = (hi.astype(F32) + pltpu.roll(mid.astype(F32), N_HEADS, axis=1)
              + pltpu.roll(lo.astype(F32), 2 * N_HEADS, axis=1))
    return packed.astype(BF16)


def _swiglu(h_bf16, w13_ref, w2_ref, chunks, side_work=None):
    acc = None
    zero = None
    for c, (c0, c1) in enumerate(chunks):
        gate = jnp.dot(h_bf16, w13_ref[:, c0:c1], preferred_element_type=F32)
        up = jnp.dot(h_bf16, w13_ref[:, D_FF + c0:D_FF + c1], preferred_element_type=F32)
        if zero is not None:
            gate = gate + zero
        zero = side_work[c]() if side_work is not None else None
        act = (gate * jax.nn.sigmoid(gate) * up).astype(BF16)
        part = jnp.dot(act, w2_ref[c0:c1, :], preferred_element_type=F32)
        acc = part if acc is None else acc + part
    return acc


def _zero_after(v):
    bits = pltpu.bitcast(v[0:V7X_SUBLANES, 0:V7X_LANES], jnp.uint32)
    bits = lax.shift_right_logical(lax.shift_right_logical(bits, jnp.uint32(16)), jnp.uint32(16))
    return pltpu.bitcast(bits, F32)[0:1, 0:1]


def _conv_rows(u_ref, s_ref, cw_ref, base):
    first = HALO - (CONV_WIDTH - 1)
    groups = CONV_ROWS // V7X_SUBLANES
    acc = None
    for kk in range(CONV_WIDTH):
        aligned, r = divmod(first + kk, V7X_SUBLANES)
        src = u_ref if r == 0 else s_ref.at[r - 1]
        lo = base + aligned * V7X_SUBLANES
        term = src[lo:lo + CONV_ROWS, :].reshape(groups, V7X_SUBLANES, D_CONV) * cw_ref[kk]
        acc = term if acc is None else acc + term
    return acc.reshape(CONV_ROWS, D_CONV)


def _load_as_bf16(jobs, stage_ref, sem):
    n_slots = stage_ref.shape[0]
    assert stage_ref.shape[1] >= LOAD_ROWS and stage_ref.shape[2] == LOAD_COLS
    assert all(rows % LOAD_ROWS == 0 and cols % LOAD_COLS == 0 for _, _, rows, cols in jobs)
    tiles = [(src, dst, r0, c0) for src, dst, rows, cols in jobs
             for r0 in range(0, rows, LOAD_ROWS) for c0 in range(0, cols, LOAD_COLS)]

    def copy(t):
        src, _, r0, c0 = tiles[t]
        slot = t % n_slots
        return pltpu.make_async_copy(src.at[pl.ds(r0, LOAD_ROWS), pl.ds(c0, LOAD_COLS)],
                                     stage_ref.at[slot, 0:LOAD_ROWS, :], sem.at[slot])

    for t in range(min(n_slots, len(tiles))):
        copy(t).start()
    for t, (_, dst, r0, c0) in enumerate(tiles):
        copy(t).wait()
        dst[r0:r0 + LOAD_ROWS, c0:c0 + LOAD_COLS] = stage_ref[t % n_slots, 0:LOAD_ROWS, :].astype(BF16)
        if t + n_slots < len(tiles):
            copy(t + n_slots).start()


def _ffn_in_body(x_ref, n1_ref, w13_hbm, w2_hbm, nm_ref, win_hbm, wf_ref, fb_ref, place_ref,
                 cq_ref, ck_ref, cw_ref, cb_ref, lg_ref, lb_ref, on_ref,
                 x1_ref, yc_ref, qt_ref, k_ref, vt_ref, eqt_ref, ek_ref,
                 carry_ref, u_ref, s_ref, cacc_ref, w13_ref, w2_ref, win_ref, load_sem,
                 *, chunks, tm, tiles_per_seq):
    n = pl.program_id(0)

    @pl.when(n == 0)
    def _():
        _load_as_bf16([(w13_hbm, w13_ref, D_MODEL, 2 * D_FF), (w2_hbm, w2_ref, D_FF, D_MODEL),
                       (win_hbm, win_ref, D_MODEL, N_IN_MAIN)], s_ref, load_sem)
        win_ref[:, N_IN_MAIN:] = wf_ref[...]
        u_ref[...] = jnp.zeros_like(u_ref)
        carry_ref[...] = jnp.zeros_like(carry_ref)

    def tile_sum(v):
        v = v.reshape(-1, V7X_SUBLANES, v.shape[-1]).sum(axis=0)
        return sum(v[:, g:g + V7X_LANES] for g in range(0, v.shape[-1], V7X_LANES))

    def shift_copies():
        for r in range(1, V7X_SUBLANES):
            s_ref[r - 1] = u_ref[r:r + s_ref.shape[1], :]
        return None

    def conv_group(bases):
        def run():
            seen = None
            for base in bases:
                rows = _conv_rows(u_ref, s_ref, cw_ref, base)
                cacc_ref[base:base + CONV_ROWS, :] = rows
                seen = tile_sum(rows) if seen is None else seen + tile_sum(rows)
            return _zero_after(seen)
        return run

    def conv_epilogue():
        cv = cacc_ref[...] + cb_ref[...]
        mu = jnp.mean(cv, axis=-1, keepdims=True)
        xc = cv - mu
        y = xc * lax.rsqrt(jnp.mean(xc * xc, axis=-1, keepdims=True) + EPS) * lg_ref[...] + lb_ref[...]
        y = y * jax.nn.sigmoid(y)
        y = _rms(y, on_ref[...])
        yc_ref[...] = y.astype(BF16)
        return _zero_after(tile_sum(y))

    bases = list(range(0, tm, CONV_ROWS))
    n_groups = len(chunks) - 3
    per_group = -(-len(bases) // n_groups)
    side_work = ([shift_copies]
                 + [conv_group(bases[g * per_group:(g + 1) * per_group]) for g in range(n_groups)]
                 + [conv_epilogue, lambda: None])

    flush_step = pl.num_programs(0) - 1

    @pl.when(n == flush_step)
    def _():
        for work in side_work:
            work()

    @pl.when(n < flush_step)
    def _():
        x = x_ref[...]
        h = _rms(x, n1_ref[...]).astype(BF16)
        x1 = x + 0.5 * _swiglu(h, w13_ref, w2_ref, chunks, side_work)
        x1_ref[...] = x1

        h2 = _rms(x1, nm_ref[...]).astype(BF16)
        proj = jnp.dot(h2, win_ref[...], preferred_element_type=F32)

        opens_sequence = lax.rem(n, tiles_per_seq) == 0
        u_ref[0:HALO, :] = jnp.where(opens_sequence, 0.0, u_ref[tm:tm + HALO, :])
        u_ref[HALO:HALO + tm, :] = proj[:, :D_CONV] * jax.nn.sigmoid(proj[:, D_CONV:2 * D_CONV])
        o = 2 * D_CONV
        qt_ref[...] = (proj[:, o:o + D_ATTN] * (HEAD_DIM ** -0.5 * LOG2E)).T.astype(BF16)
        k_ref[...] = proj[:, o + D_ATTN:o + 2 * D_ATTN].astype(BF16)
        vt_ref[...] = proj[:, o + 2 * D_ATTN:o + 3 * D_ATTN].T.astype(BF16)

        z = proj[:, N_IN_MAIN:] + fb_ref[...]
        logf = jnp.minimum(z, 0.0) - jnp.log1p(jnp.exp(-jnp.abs(z)))
        lane = lax.broadcasted_iota(jnp.int32, logf.shape, 1)
        logf = jnp.where(lane < N_HEADS, logf, 0.0)

        row = lax.broadcasted_iota(jnp.int32, (tm, tm), 0)
        col = lax.broadcasted_iota(jnp.int32, (tm, tm), 1)
        tril = jnp.where(row >= col, 1.0, 0.0).astype(BF16)
        d_packed = jnp.dot(tril, _pack3(logf), preferred_element_type=F32)
        d_loc = (d_packed + pltpu.roll(d_packed, F_PAD - N_HEADS, axis=1)
                 + pltpu.roll(d_packed, F_PAD - 2 * N_HEADS, axis=1))
        d_loc = jnp.where(lane < N_HEADS, d_loc, 0.0)

        d = d_loc + jnp.where(opens_sequence, 0.0, carry_ref[0:1, :])
        carry_ref[0:1, :] = d[tm - 1:tm, :]

        placed = jnp.dot(_pack3(d * LOG2E), place_ref[...], preferred_element_type=F32)
        eqt_ref[...] = (placed[:, :D_ATTN] + cq_ref[...]).T.astype(BF16)
        ek_ref[...] = (placed[:, D_ATTN:] + ck_ref[...]).astype(BF16)


def _attn_body(qt_ref, eqt_ref, k_ref, ek_ref, vt_ref, wa_ref, wb_ref, wc_ref,
               o_ref, wa_out, wb_out, wc_out, ka_ref, va_ref, *, seq, tq):
    for w_in_ref, w_out_ref in ((wa_ref, wa_out), (wb_ref, wb_out), (wc_ref, wc_out)):
        w_out_ref[...] = w_in_ref[...].astype(BF16)

    ka_ref[:, 0:PAIR] = k_ref[...]
    ka_ref[:, PAIR:2 * PAIR] = ek_ref[...]
    for hh in range(2):
        va_ref[hh, 0:HEAD_DIM, :] = vt_ref[hh * HEAD_DIM:(hh + 1) * HEAD_DIM, :]
        va_ref[hh, HEAD_DIM:HEAD_DIM + ONES_ROWS, :] = jnp.ones((ONES_ROWS, seq), BF16)

    key = lax.broadcasted_iota(jnp.int32, (tq, 2 * tq), 0)
    qry = lax.broadcasted_iota(jnp.int32, (tq, 2 * tq), 1)
    causal = key <= jnp.where(qry < tq, qry, qry - tq)
    zeros = jnp.zeros((HEAD_DIM, tq), BF16)

    def masked_queries(i):
        q, e = qt_ref[:, i * tq:(i + 1) * tq], eqt_ref[:, i * tq:(i + 1) * tq]
        h0 = jnp.concatenate([q[0:HEAD_DIM], zeros, e[0:HEAD_DIM], zeros], axis=0)
        h1 = jnp.concatenate([zeros, q[HEAD_DIM:PAIR], zeros, e[HEAD_DIM:PAIR]], axis=0)
        return jnp.concatenate([h0, h1], axis=1)

    qm = {}

    def scores(i, j, w):
        if i not in qm:
            qm.clear()
            qm[i] = masked_queries(i)
        s = jnp.dot(ka_ref[j * tq:(j + w) * tq, :], qm[i], preferred_element_type=F32)
        if j + w <= i:
            return s
        diag = jnp.where(causal, s[(w - 1) * tq:, :], -jnp.inf)
        return diag if w == 1 else jnp.concatenate([s[0:(w - 1) * tq, :], diag], axis=0)

    tasks = [(i, j, min(KEY_BLOCKS, i + 1 - j))
             for i in range(seq // tq) for j in range(0, i + 1, KEY_BLOCKS)]
    s_next = scores(*tasks[0])
    m = acc = None
    for n, (i, j, w) in enumerate(tasks):
        s = s_next
        if n + 1 < len(tasks):
            s_next = scores(*tasks[n + 1])
        m_blk = jnp.max(s, axis=0, keepdims=True)
        m_new = m_blk if j == 0 else jnp.maximum(m, m_blk)
        p = jnp.exp2(s - m_new).astype(BF16)
        o = [jnp.dot(va_ref[hh, :, j * tq:(j + w) * tq], p[:, hh * tq:(hh + 1) * tq],
                     preferred_element_type=F32) for hh in range(2)]
        if j == 0:
            acc = o
        else:
            alpha = jnp.exp2(m - m_new)
            acc = [acc[hh] * alpha[:, hh * tq:(hh + 1) * tq] + o[hh] for hh in range(2)]
        m = m_new
        if j + w == i + 1:
            heads = [a[0:HEAD_DIM, :] * (1.0 / a[HEAD_DIM:HEAD_DIM + 1, :]) for a in acc]
            o_ref[i * tq:(i + 1) * tq, :] = jnp.concatenate(heads, axis=0).T


def _out_ffn_body(x1_ref, yc_ref, ya_ref, na_ref, wo_ref, n2_ref, w13_ref, w2_ref, nf_ref,
                  o_ref, *, chunks):
    ya = _rms(ya_ref[...], na_ref[...]).astype(BF16)
    y = (jnp.dot(yc_ref[...], wo_ref[0:D_CONV, :], preferred_element_type=F32)
         + jnp.dot(ya, wo_ref[D_CONV:D_MIX, :], preferred_element_type=F32))
    x2 = x1_ref[...] + y
    h = _rms(x2, n2_ref[...]).astype(BF16)
    x3 = x2 + 0.5 * _swiglu(h, w13_ref, w2_ref, chunks)
    o_ref[...] = _rms(x3, nf_ref[...])


def _resident(shape):
    nd = len(shape)
    return pl.BlockSpec(shape, lambda *_: (0,) * nd, pipeline_mode=pl.Buffered(1))


def _placement():
    place = np.zeros((F_PAD, 2 * D_ATTN), np.float32)
    cq = np.zeros((1, D_ATTN), np.float32)
    ck = np.zeros((1, D_ATTN), np.float32)
    for hd in range(N_HEADS):
        base = hd * HEAD_DIM
        for i in range(3):
            place[i * N_HEADS + hd, base + i] = 1.0
            place[i * N_HEADS + hd, D_ATTN + base + 3 + i] = -1.0
            cq[0, base + 3 + i] = 1.0
            ck[0, base + i] = 1.0
    return jnp.asarray(place, BF16), jnp.asarray(cq), jnp.asarray(ck)


def _layer(x, p, final_norm, plan):
    bsz, seq, _ = x.shape
    tm, tq = plan["tm"], plan["tq"]
    chunks = plan["ffn_chunks"]
    cparams = functools.partial(pltpu.CompilerParams, vmem_limit_bytes=plan["vmem_limit"])
    row2 = lambda v: v.reshape(1, -1)

    w_f = jnp.pad(p["w_in"][:, N_IN_MAIN:].astype(BF16), ((0, 0), (0, F_PAD - N_HEADS)))
    in_hbm = pl.BlockSpec(memory_space=pl.ANY)
    fb = jnp.pad(p["forget_b"], (0, F_PAD - N_HEADS)).reshape(1, F_PAD)
    place, cq, ck = _placement()

    tiles_per_seq = seq // tm
    n_tiles = bsz * tiles_per_seq
    cur = lambda n: jnp.minimum(n, n_tiles - 1)
    flat = lambda width: pl.BlockSpec((tm, width), lambda n: (cur(n), 0))
    flat_t = pl.BlockSpec((None, D_ATTN, tm),
                          lambda n: (cur(n) // tiles_per_seq, 0, cur(n) % tiles_per_seq))
    lagged = pl.BlockSpec((tm, D_CONV), lambda n: (jnp.maximum(n - 1, 0), 0))
    attn_t = jax.ShapeDtypeStruct((bsz, D_ATTN, seq), BF16)
    attn_n = jax.ShapeDtypeStruct((bsz * seq, D_ATTN), BF16)
    x1, yc, qt, k, vt, eqt, ek = pl.pallas_call(
        functools.partial(_ffn_in_body, chunks=plan["ffn_in_chunks"], tm=tm,
                          tiles_per_seq=tiles_per_seq),
        grid=(n_tiles + 1,),
        in_specs=[flat(D_MODEL), _resident((1, D_MODEL)), in_hbm, in_hbm, _resident((1, D_MODEL)),
                  in_hbm, _resident((D_MODEL, F_PAD)),
                  _resident((1, F_PAD)), _resident((F_PAD, 2 * D_ATTN)),
                  _resident((1, D_ATTN)), _resident((1, D_ATTN)),
                  _resident((CONV_WIDTH, V7X_SUBLANES, D_CONV))] + [_resident((1, D_CONV))] * 4,
        out_specs=[flat(D_MODEL), lagged, flat_t, flat(D_ATTN), flat_t, flat_t, flat(D_ATTN)],
        out_shape=[jax.ShapeDtypeStruct((bsz * seq, D_MODEL), F32),
                   jax.ShapeDtypeStruct((bsz * seq, D_CONV), BF16),
                   attn_t, attn_n, attn_t, attn_t, attn_n],
        scratch_shapes=[pltpu.VMEM((V7X_SUBLANES, F_PAD), F32),
                        pltpu.VMEM((HALO + tm, D_CONV), F32),
                        pltpu.VMEM((V7X_SUBLANES - 1, HALO + tm - V7X_SUBLANES, D_CONV), F32),
                        pltpu.VMEM((tm, D_CONV), F32),
                        pltpu.VMEM((D_MODEL, 2 * D_FF), BF16),
                        pltpu.VMEM((D_FF, D_MODEL), BF16),
                        pltpu.VMEM((D_MODEL, N_IN_MAIN + F_PAD), BF16),
                        pltpu.SemaphoreType.DMA((V7X_SUBLANES - 1,))],
        compiler_params=cparams(dimension_semantics=("arbitrary",)),
        name="ffn_in",
    )(x.reshape(bsz * seq, D_MODEL), row2(p["ffn1_norm"]), p["ffn1_w13"], p["ffn1_w2"],
      row2(p["mix_norm"]), p["w_in"], w_f, fb, place, cq, ck,
      jnp.broadcast_to(p["conv_w"][:, None, :], (CONV_WIDTH, V7X_SUBLANES, D_CONV)),
      row2(p["conv_b"]), row2(p["conv_ln_g"]), row2(p["conv_ln_b"]), row2(p["out_norm_conv"]))
    x1 = x1.reshape(bsz, seq, D_MODEL)
    yc = yc.reshape(bsz, seq, D_CONV)
    k = k.reshape(bsz, seq, D_ATTN)
    ek = ek.reshape(bsz, seq, D_ATTN)

    pair = pl.BlockSpec((None, seq, PAIR), lambda b, g: (b, 0, g))
    pair_t = pl.BlockSpec((None, PAIR, seq), lambda b, g: (b, g, 0))
    steps = bsz * N_PAIRS

    def n_slabs(w):
        return max(d for d in range(1, steps + 1)
                   if steps % d == 0 and w.shape[0] % (d * BF16_SUBLANES) == 0)

    slab = lambda w: w.reshape(n_slabs(w), w.shape[0] // n_slabs(w), w.shape[1])
    slab_spec = lambda w: pl.BlockSpec(
        (None,) + slab(w).shape[1:],
        lambda b, g, per=steps // n_slabs(w): ((b * N_PAIRS + g) // per, 0, 0))
    late_w = [p["w_out"], p["ffn2_w13"], p["ffn2_w2"]]
    ya, w_out, w13_2, w2_2 = pl.pallas_call(
        functools.partial(_attn_body, seq=seq, tq=tq),
        grid=(bsz, N_PAIRS),
        in_specs=[pair_t, pair_t, pair, pair, pair_t] + [slab_spec(w) for w in late_w],
        out_specs=[pair] + [slab_spec(w) for w in late_w],
        out_shape=[jax.ShapeDtypeStruct((bsz, seq, D_ATTN), F32)]
                  + [jax.ShapeDtypeStruct(slab(w).shape, BF16) for w in late_w],
        scratch_shapes=[pltpu.VMEM((seq, 2 * PAIR), BF16),
                        pltpu.VMEM((2, HEAD_DIM + ONES_ROWS, seq), BF16)],
        compiler_params=cparams(dimension_semantics=("arbitrary", "arbitrary")),
        name="attn",
    )(qt, eqt, k, ek, vt, *[slab(w) for w in late_w])
    w_out, w13_2, w2_2 = [w.reshape(src.shape) for w, src in zip((w_out, w13_2, w2_2), late_w)]

    tmo = plan["tm_out"]
    tok = lambda width: pl.BlockSpec((None, tmo, width), lambda b, s: (b, s, 0))
    out = pl.pallas_call(
        functools.partial(_out_ffn_body, chunks=chunks),
        grid=(bsz, seq // tmo),
        in_specs=[tok(D_MODEL), tok(D_CONV), tok(D_ATTN), _resident((1, D_ATTN)),
                  _resident((D_MIX, D_MODEL)), _resident((1, D_MODEL)),
                  _resident((D_MODEL, 2 * D_FF)), _resident((D_FF, D_MODEL)),
                  _resident((1, D_MODEL))],
        out_specs=tok(D_MODEL),
        out_shape=jax.ShapeDtypeStruct((bsz, seq, D_MODEL), F32),
        compiler_params=cparams(dimension_semantics=("arbitrary", "arbitrary")),
        name="out_ffn",
    )(x1, yc, ya, row2(p["out_norm_attn"]), w_out, row2(p["ffn2_norm"]), w13_2, w2_2,
      row2(final_norm))
    return out


def kernel(x, ffn1_norm, ffn1_w13, ffn1_w2, mix_norm, w_in, conv_w, conv_b, conv_ln_g, conv_ln_b,
           forget_b, out_norm_conv, out_norm_attn, w_out, ffn2_norm, ffn2_w13, ffn2_w2, final_norm):
    depth = ffn1_norm.shape[0]
    assert depth == 1, "the fused final RMSNorm assumes a single layer"
    plan = _plan()
    stacked = dict(ffn1_norm=ffn1_norm, ffn1_w13=ffn1_w13, ffn1_w2=ffn1_w2, mix_norm=mix_norm,
                   w_in=w_in, conv_w=conv_w, conv_b=conv_b, conv_ln_g=conv_ln_g,
                   conv_ln_b=conv_ln_b, forget_b=forget_b, out_norm_conv=out_norm_conv,
                   out_norm_attn=out_norm_attn, w_out=w_out, ffn2_norm=ffn2_norm,
                   ffn2_w13=ffn2_w13, ffn2_w2=ffn2_w2)
    layer = {name: arr[0] for name, arr in stacked.items()}
    return _layer(x, layer, final_norm, plan)
```

```python
import functools

import numpy as np
import jax
import jax.numpy as jnp
from jax import lax
from jax.experimental import pallas as pl
from jax.experimental.pallas import tpu as pltpu

D_MODEL = 1024
D_CONV = 512
CONV_WIDTH = 31
N_HEADS = 8
HEAD_DIM = 64
D_ATTN = N_HEADS * HEAD_DIM
D_MIX = D_CONV + D_ATTN
D_FF = 2816
EPS = 1e-6
LOG2E = 1.4426950408889634

V7X_LANES = 128
V7X_SUBLANES = 8
BF16_SUBLANES = 16
V7X_MXU_DIM = 256
V7X_VMEM_BYTES = 64 * 1024 * 1024

PAIR = 2 * HEAD_DIM
N_PAIRS = N_HEADS // 2
F_PAD = V7X_LANES
N_IN_MAIN = 2 * D_CONV + 3 * D_ATTN
HALO = 32
CONV_ROWS = 16
LOAD_ROWS, LOAD_COLS = 256, 512
KEY_BLOCKS = 2
ONES_ROWS = 16

BF16 = jnp.bfloat16
F32 = jnp.float32


def _plan():
    def hidden_chunks(mxu_tiles):
        step = mxu_tiles * V7X_MXU_DIM
        return tuple((c, min(c + step, D_FF)) for c in range(0, D_FF, step))

    return dict(
        tm=512,
        tm_out=1024,
        tq=256,
        ffn_chunks=hidden_chunks(6),
        ffn_in_chunks=hidden_chunks(1),
        vmem_limit=V7X_VMEM_BYTES - 8 * 1024 * 1024,
    )


def _rms(x, g):
    return x * lax.rsqrt(jnp.mean(x * x, axis=-1, keepdims=True) + EPS) * g


def _split3(x):
    hi = x.astype(BF16)
    r = x - hi.astype(F32)
    mid = r.astype(BF16)
    lo = (r - mid.astype(F32)).astype(BF16)
    return hi, mid, lo


def _pack3(x):
    hi, mid, lo = _split3(x)
    packed = (hi.astype(F32) + pltpu.roll(mid.astype(F32), N_HEADS, axis=1)
              + pltpu.roll(lo.astype(F32), 2 * N_HEADS, axis=1))
    return packed.astype(BF16)


def _swiglu(h_bf16, w13_ref, w2_ref, chunks, side_work=None):
    acc = None
    zero = None
    for c, (c0, c1) in enumerate(chunks):
        gate = jnp.dot(h_bf16, w13_ref[:, c0:c1], preferred_element_type=F32)
        up = jnp.dot(h_bf16, w13_ref[:, D_FF + c0:D_FF + c1], preferred_element_type=F32)
        if zero is not None:
            gate = gate + zero
        zero = side_work[c]() if side_work is not None else None
        act = (gate * jax.nn.sigmoid(gate) * up).astype(BF16)
        part = jnp.dot(act, w2_ref[c0:c1, :], preferred_element_type=F32)
        acc = part if acc is None else acc + part
    return acc


def _zero_after(v):
    bits = pltpu.bitcast(v[0:V7X_SUBLANES, 0:V7X_LANES], jnp.uint32)
    bits = lax.shift_right_logical(lax.shift_right_logical(bits, jnp.uint32(16)), jnp.uint32(16))
    return pltpu.bitcast(bits, F32)[0:1, 0:1]


def _conv_rows(u_ref, s_ref, cw_ref, base):
    first = HALO - (CONV_WIDTH - 1)
    groups = CONV_ROWS // V7X_SUBLANES
    acc = None
    for kk in range(CONV_WIDTH):
        aligned, r = divmod(first + kk, V7X_SUBLANES)
        src = u_ref if r == 0 else s_ref.at[r - 1]
        lo = base + aligned * V7X_SUBLANES
        term = src[lo:lo + CONV_ROWS, :].reshape(groups, V7X_SUBLANES, D_CONV) * cw_ref[kk]
        acc = term if acc is None else acc + term
    return acc.reshape(CONV_ROWS, D_CONV)


def _load_as_bf16(jobs, stage_ref, sem):
    n_slots = stage_ref.shape[0]
    assert stage_ref.shape[1] >= max(LOAD_ROWS, LOAD_COLS) and stage_ref.shape[2] == LOAD_COLS
    assert all(rows % LOAD_ROWS == 0 and cols % LOAD_COLS == 0 for _, _, rows, cols, _ in jobs)
    tiles = [(src, dst, r0, c0, tr) for src, dst, rows, cols, tr in jobs
             for r0 in range(0, rows, LOAD_ROWS) for c0 in range(0, cols, LOAD_COLS)]

    def stage(t):
        transposed = tiles[t][4]
        shape = (LOAD_COLS, LOAD_ROWS) if transposed else (LOAD_ROWS, LOAD_COLS)
        return stage_ref.at[t % n_slots, 0:shape[0], 0:shape[1]]

    def copy(t):
        src, _, r0, c0, transposed = tiles[t]
        window = (src.at[pl.ds(c0, LOAD_COLS), pl.ds(r0, LOAD_ROWS)] if transposed
                  else src.at[pl.ds(r0, LOAD_ROWS), pl.ds(c0, LOAD_COLS)])
        return pltpu.make_async_copy(window, stage(t), sem.at[t % n_slots])

    for t in range(min(n_slots, len(tiles))):
        copy(t).start()
    for t, (_, dst, r0, c0, transposed) in enumerate(tiles):
        copy(t).wait()
        tile = stage(t)[...]
        dst[r0:r0 + LOAD_ROWS, c0:c0 + LOAD_COLS] = (tile.T if transposed else tile).astype(BF16)
        if t + n_slots < len(tiles):
            copy(t + n_slots).start()


def _ffn_in_body(x_ref, n1_ref, w13_hbm, w2_hbm, nm_ref, wint_hbm, wf_ref, fb_ref, place_ref,
                 cq_ref, ck_ref, cw_ref, cb_ref, lg_ref, lb_ref, on_ref,
                 x1_ref, yc_ref, qt_ref, k_ref, vt_ref, eqt_ref, ek_ref,
                 carry_ref, u_ref, s_ref, cacc_ref, w13_ref, w2_ref, win_ref, load_sem,
                 *, chunks, tm, tiles_per_seq):
    n = pl.program_id(0)

    @pl.when(n == 0)
    def _():
        _load_as_bf16([(w13_hbm, w13_ref, D_MODEL, 2 * D_FF, False),
                       (w2_hbm, w2_ref, D_FF, D_MODEL, False),
                       (wint_hbm, win_ref, D_MODEL, N_IN_MAIN, True)], s_ref, load_sem)
        win_ref[:, N_IN_MAIN:] = wf_ref[...].T.astype(BF16)
        u_ref[...] = jnp.zeros_like(u_ref)
        carry_ref[...] = jnp.zeros_like(carry_ref)

    def tile_sum(v):
        v = v.reshape(-1, V7X_SUBLANES, v.shape[-1]).sum(axis=0)
        return sum(v[:, g:g + V7X_LANES] for g in range(0, v.shape[-1], V7X_LANES))

    def shift_copies():
        for r in range(1, V7X_SUBLANES):
            s_ref[r - 1] = u_ref[r:r + s_ref.shape[1], :]
        return None

    def conv_group(bases):
        def run():
            seen = None
            for base in bases:
                rows = _conv_rows(u_ref, s_ref, cw_ref, base)
                cacc_ref[base:base + CONV_ROWS, :] = rows
                seen = tile_sum(rows) if seen is None else seen + tile_sum(rows)
            return _zero_after(seen)
        return run

    def conv_epilogue():
        cv = cacc_ref[...] + cb_ref[...]
        mu = jnp.mean(cv, axis=-1, keepdims=True)
        xc = cv - mu
        y = xc * lax.rsqrt(jnp.mean(xc * xc, axis=-1, keepdims=True) + EPS) * lg_ref[...] + lb_ref[...]
        y = y * jax.nn.sigmoid(y)
        y = _rms(y, on_ref[...])
        yc_ref[...] = y.astype(BF16)
        return _zero_after(tile_sum(y))

    bases = list(range(0, tm, CONV_ROWS))
    n_groups = len(chunks) - 3
    per_group = -(-len(bases) // n_groups)
    side_work = ([shift_copies]
                 + [conv_group(bases[g * per_group:(g + 1) * per_group]) for g in range(n_groups)]
                 + [conv_epilogue, lambda: None])

    flush_step = pl.num_programs(0) - 1

    @pl.when(n == flush_step)
    def _():
        for work in side_work:
            work()

    @pl.when(n < flush_step)
    def _():
        x = x_ref[...]
        h = _rms(x, n1_ref[...]).astype(BF16)
        x1 = x + 0.5 * _swiglu(h, w13_ref, w2_ref, chunks, side_work)
        x1_ref[...] = x1

        h2 = _rms(x1, nm_ref[...]).astype(BF16)
        proj = jnp.dot(h2, win_ref[...], preferred_element_type=F32)

        opens_sequence = lax.rem(n, tiles_per_seq) == 0
        u_ref[0:HALO, :] = jnp.where(opens_sequence, 0.0, u_ref[tm:tm + HALO, :])
        u_ref[HALO:HALO + tm, :] = proj[:, :D_CONV] * jax.nn.sigmoid(proj[:, D_CONV:2 * D_CONV])
        o = 2 * D_CONV
        qt_ref[...] = (proj[:, o:o + D_ATTN] * (HEAD_DIM ** -0.5 * LOG2E)).T.astype(BF16)
        k_ref[...] = proj[:, o + D_ATTN:o + 2 * D_ATTN].astype(BF16)
        vt_ref[...] = proj[:, o + 2 * D_ATTN:o + 3 * D_ATTN].T.astype(BF16)

        z = proj[:, N_IN_MAIN:] + fb_ref[...]
        logf = jnp.minimum(z, 0.0) - jnp.log1p(jnp.exp(-jnp.abs(z)))
        lane = lax.broadcasted_iota(jnp.int32, logf.shape, 1)
        logf = jnp.where(lane < N_HEADS, logf, 0.0)

        row = lax.broadcasted_iota(jnp.int32, (tm, tm), 0)
        col = lax.broadcasted_iota(jnp.int32, (tm, tm), 1)
        tril = jnp.where(row >= col, 1.0, 0.0).astype(BF16)
        d_packed = jnp.dot(tril, _pack3(logf), preferred_element_type=F32)
        d_loc = (d_packed + pltpu.roll(d_packed, F_PAD - N_HEADS, axis=1)
                 + pltpu.roll(d_packed, F_PAD - 2 * N_HEADS, axis=1))
        d_loc = jnp.where(lane < N_HEADS, d_loc, 0.0)

        d = d_loc + jnp.where(opens_sequence, 0.0, carry_ref[0:1, :])
        carry_ref[0:1, :] = d[tm - 1:tm, :]

        placed = jnp.dot(_pack3(d * LOG2E), place_ref[...], preferred_element_type=F32)
        eqt_ref[...] = (placed[:, :D_ATTN] + cq_ref[...]).T.astype(BF16)
        ek_ref[...] = (placed[:, D_ATTN:] + ck_ref[...]).astype(BF16)


def _attn_body(qt_ref, eqt_ref, k_ref, ek_ref, vt_ref, wa_ref, wb_ref, wc_ref,
               o_ref, wa_out, wb_out, wc_out, ka_ref, va_ref, *, seq, tq):
    for w_in_ref, w_out_ref in ((wa_ref, wa_out), (wb_ref, wb_out), (wc_ref, wc_out)):
        w_out_ref[...] = w_in_ref[...].astype(BF16)

    ka_ref[:, 0:PAIR] = k_ref[...]
    ka_ref[:, PAIR:2 * PAIR] = ek_ref[...]
    for hh in range(2):
        va_ref[hh, 0:HEAD_DIM, :] = vt_ref[hh * HEAD_DIM:(hh + 1) * HEAD_DIM, :]
        va_ref[hh, HEAD_DIM:HEAD_DIM + ONES_ROWS, :] = jnp.ones((ONES_ROWS, seq), BF16)

    key = lax.broadcasted_iota(jnp.int32, (tq, 2 * tq), 0)
    qry = lax.broadcasted_iota(jnp.int32, (tq, 2 * tq), 1)
    causal = key <= jnp.where(qry < tq, qry, qry - tq)
    zeros = jnp.zeros((HEAD_DIM, tq), BF16)

    def masked_queries(i):
        q, e = qt_ref[:, i * tq:(i + 1) * tq], eqt_ref[:, i * tq:(i + 1) * tq]
        h0 = jnp.concatenate([q[0:HEAD_DIM], zeros, e[0:HEAD_DIM], zeros], axis=0)
        h1 = jnp.concatenate([zeros, q[HEAD_DIM:PAIR], zeros, e[HEAD_DIM:PAIR]], axis=0)
        return jnp.concatenate([h0, h1], axis=1)

    qm = {}

    def scores(i, j, w):
        if i not in qm:
            qm.clear()
            qm[i] = masked_queries(i)
        s = jnp.dot(ka_ref[j * tq:(j + w) * tq, :], qm[i], preferred_element_type=F32)
        if j + w <= i:
            return s
        diag = jnp.where(causal, s[(w - 1) * tq:, :], -jnp.inf)
        return diag if w == 1 else jnp.concatenate([s[0:(w - 1) * tq, :], diag], axis=0)

    tasks = [(i, j, min(KEY_BLOCKS, i + 1 - j))
             for i in range(seq // tq) for j in range(0, i + 1, KEY_BLOCKS)]
    s_next = scores(*tasks[0])
    m = acc = None
    for n, (i, j, w) in enumerate(tasks):
        s = s_next
        if n + 1 < len(tasks):
            s_next = scores(*tasks[n + 1])
        m_blk = jnp.max(s, axis=0, keepdims=True)
        m_new = m_blk if j == 0 else jnp.maximum(m, m_blk)
        p = jnp.exp2(s - m_new).astype(BF16)
        o = [jnp.dot(va_ref[hh, :, j * tq:(j + w) * tq], p[:, hh * tq:(hh + 1) * tq],
                     preferred_element_type=F32) for hh in range(2)]
        if j == 0:
            acc = o
        else:
            alpha = jnp.exp2(m - m_new)
            acc = [acc[hh] * alpha[:, hh * tq:(hh + 1) * tq] + o[hh] for hh in range(2)]
        m = m_new
        if j + w == i + 1:
            heads = [a[0:HEAD_DIM, :] * (1.0 / a[HEAD_DIM:HEAD_DIM + 1, :]) for a in acc]
            o_ref[i * tq:(i + 1) * tq, :] = jnp.concatenate(heads, axis=0).T


def _out_ffn_body(x1_ref, yc_ref, ya_ref, na_ref, wo_ref, n2_ref, w13_ref, w2_ref, nf_ref,
                  o_ref, *, chunks):
    ya = _rms(ya_ref[...], na_ref[...]).astype(BF16)
    y = jnp.dot(jnp.concatenate([yc_ref[...], ya], axis=1), wo_ref[...], preferred_element_type=F32)
    x2 = x1_ref[...] + y
    h = _rms(x2, n2_ref[...]).astype(BF16)
    x3 = x2 + 0.5 * _swiglu(h, w13_ref, w2_ref, chunks)
    o_ref[...] = _rms(x3, nf_ref[...])


def _resident(shape):
    nd = len(shape)
    return pl.BlockSpec(shape, lambda *_: (0,) * nd, pipeline_mode=pl.Buffered(1))


def _placement():
    place = np.zeros((F_PAD, 2 * D_ATTN), np.float32)
    cq = np.zeros((1, D_ATTN), np.float32)
    ck = np.zeros((1, D_ATTN), np.float32)
    for hd in range(N_HEADS):
        base = hd * HEAD_DIM
        for i in range(3):
            place[i * N_HEADS + hd, base + i] = 1.0
            place[i * N_HEADS + hd, D_ATTN + base + 3 + i] = -1.0
            cq[0, base + 3 + i] = 1.0
            ck[0, base + i] = 1.0
    return jnp.asarray(place, BF16), jnp.asarray(cq), jnp.asarray(ck)


def _layer(x, p, final_norm, plan):
    bsz, seq, _ = x.shape
    tm, tq = plan["tm"], plan["tq"]
    chunks = plan["ffn_chunks"]
    cparams = functools.partial(pltpu.CompilerParams, vmem_limit_bytes=plan["vmem_limit"])
    row2 = lambda v: v.reshape(1, -1)

    w_in_t = jnp.swapaxes(p["w_in"], 0, 1)
    w_f_t = jnp.pad(w_in_t[N_IN_MAIN:], ((0, F_PAD - N_HEADS), (0, 0)))
    in_hbm = pl.BlockSpec(memory_space=pl.ANY)
    fb = jnp.pad(p["forget_b"], (0, F_PAD - N_HEADS)).reshape(1, F_PAD)
    place, cq, ck = _placement()

    tiles_per_seq = seq // tm
    n_tiles = bsz * tiles_per_seq
    cur = lambda n: jnp.minimum(n, n_tiles - 1)
    flat = lambda width: pl.BlockSpec((tm, width), lambda n: (cur(n), 0))
    flat_t = pl.BlockSpec((None, D_ATTN, tm),
                          lambda n: (cur(n) // tiles_per_seq, 0, cur(n) % tiles_per_seq))
    lagged = pl.BlockSpec((tm, D_CONV), lambda n: (jnp.maximum(n - 1, 0), 0))
    attn_t = jax.ShapeDtypeStruct((bsz, D_ATTN, seq), BF16)
    attn_n = jax.ShapeDtypeStruct((bsz * seq, D_ATTN), BF16)
    x1, yc, qt, k, vt, eqt, ek = pl.pallas_call(
        functools.partial(_ffn_in_body, chunks=plan["ffn_in_chunks"], tm=tm,
                          tiles_per_seq=tiles_per_seq),
        grid=(n_tiles + 1,),
        in_specs=[flat(D_MODEL), _resident((1, D_MODEL)), in_hbm, in_hbm, _resident((1, D_MODEL)),
                  in_hbm, _resident((F_PAD, D_MODEL)),
                  _resident((1, F_PAD)), _resident((F_PAD, 2 * D_ATTN)),
                  _resident((1, D_ATTN)), _resident((1, D_ATTN)),
                  _resident((CONV_WIDTH, V7X_SUBLANES, D_CONV))] + [_resident((1, D_CONV))] * 4,
        out_specs=[flat(D_MODEL), lagged, flat_t, flat(D_ATTN), flat_t, flat_t, flat(D_ATTN)],
        out_shape=[jax.ShapeDtypeStruct((bsz * seq, D_MODEL), F32),
                   jax.ShapeDtypeStruct((bsz * seq, D_CONV), BF16),
                   attn_t, attn_n, attn_t, attn_t, attn_n],
        scratch_shapes=[pltpu.VMEM((V7X_SUBLANES, F_PAD), F32),
                        pltpu.VMEM((HALO + tm, D_CONV), F32),
                        pltpu.VMEM((V7X_SUBLANES - 1, HALO + tm - V7X_SUBLANES, D_CONV), F32),
                        pltpu.VMEM((tm, D_CONV), F32),
                        pltpu.VMEM((D_MODEL, 2 * D_FF), BF16),
                        pltpu.VMEM((D_FF, D_MODEL), BF16),
                        pltpu.VMEM((D_MODEL, N_IN_MAIN + F_PAD), BF16),
                        pltpu.SemaphoreType.DMA((V7X_SUBLANES - 1,))],
        compiler_params=cparams(dimension_semantics=("arbitrary",)),
        name="ffn_in",
    )(x.reshape(bsz * seq, D_MODEL), row2(p["ffn1_norm"]), p["ffn1_w13"], p["ffn1_w2"],
      row2(p["mix_norm"]), w_in_t, w_f_t, fb, place, cq, ck,
      jnp.broadcast_to(p["conv_w"][:, None, :], (CONV_WIDTH, V7X_SUBLANES, D_CONV)),
      row2(p["conv_b"]), row2(p["conv_ln_g"]), row2(p["conv_ln_b"]), row2(p["out_norm_conv"]))
    x1 = x1.reshape(bsz, seq, D_MODEL)
    yc = yc.reshape(bsz, seq, D_CONV)
    k = k.reshape(bsz, seq, D_ATTN)
    ek = ek.reshape(bsz, seq, D_ATTN)

    pair = pl.BlockSpec((None, seq, PAIR), lambda b, g: (b, 0, g))
    pair_t = pl.BlockSpec((None, PAIR, seq), lambda b, g: (b, g, 0))
    steps = bsz * N_PAIRS

    def n_slabs(w):
        return max(d for d in range(1, steps + 1)
                   if steps % d == 0 and w.shape[0] % (d * BF16_SUBLANES) == 0)

    slab = lambda w: w.reshape(n_slabs(w), w.shape[0] // n_slabs(w), w.shape[1])
    slab_spec = lambda w: pl.BlockSpec(
        (None,) + slab(w).shape[1:],
        lambda b, g, per=steps // n_slabs(w): ((b * N_PAIRS + g) // per, 0, 0))
    late_w = [p["w_out"], p["ffn2_w13"], p["ffn2_w2"]]
    ya, w_out, w13_2, w2_2 = pl.pallas_call(
        functools.partial(_attn_body, seq=seq, tq=tq),
        grid=(bsz, N_PAIRS),
        in_specs=[pair_t, pair_t, pair, pair, pair_t] + [slab_spec(w) for w in late_w],
        out_specs=[pair] + [slab_spec(w) for w in late_w],
        out_shape=[jax.ShapeDtypeStruct((bsz, seq, D_ATTN), F32)]
                  + [jax.ShapeDtypeStruct(slab(w).shape, BF16) for w in late_w],
        scratch_shapes=[pltpu.VMEM((seq, 2 * PAIR), BF16),
                        pltpu.VMEM((2, HEAD_DIM + ONES_ROWS, seq), BF16)],
        compiler_params=cparams(dimension_semantics=("arbitrary", "arbitrary")),
        name="attn",
    )(qt, eqt, k, ek, vt, *[slab(w) for w in late_w])
    w_out, w13_2, w2_2 = [w.reshape(src.shape) for w, src in zip((w_out, w13_2, w2_2), late_w)]

    tmo = plan["tm_out"]
    tok = lambda width: pl.BlockSpec((None, tmo, width), lambda b, s: (b, s, 0))
    out = pl.pallas_call(
        functools.partial(_out_ffn_body, chunks=chunks),
        grid=(bsz, seq // tmo),
        in_specs=[tok(D_MODEL), tok(D_CONV), tok(D_ATTN), _resident((1, D_ATTN)),
                  _resident((D_MIX, D_MODEL)), _resident((1, D_MODEL)),
                  _resident((D_MODEL, 2 * D_FF)), _resident((D_FF, D_MODEL)),
                  _resident((1, D_MODEL))],
        out_specs=tok(D_MODEL),
        out_shape=jax.ShapeDtypeStruct((bsz, seq, D_MODEL), F32),
        compiler_params=cparams(dimension_semantics=("arbitrary", "arbitrary")),
        name="out_ffn",
    )(x1, yc, ya, row2(p["out_norm_attn"]), w_out, row2(p["ffn2_norm"]), w13_2, w2_2,
      row2(final_norm))
    return out


def kernel(x, ffn1_norm, ffn1_w13, ffn1_w2, mix_norm, w_in, conv_w, conv_b, conv_ln_g, conv_ln_b,
           forget_b, out_norm_conv, out_norm_attn, w_out, ffn2_norm, ffn2_w13, ffn2_w2, final_norm):
    depth = ffn1_norm.shape[0]
    assert depth == 1, "the fused final RMSNorm assumes a single layer"
    plan = _plan()
    stacked = dict(ffn1_norm=ffn1_norm, ffn1_w13=ffn1_w13, ffn1_w2=ffn1_w2, mix_norm=mix_norm,
                   w_in=w_in, conv_w=conv_w, conv_b=conv_b, conv_ln_g=conv_ln_g,
                   conv_ln_b=conv_ln_b, forget_b=forget_b, out_norm_conv=out_norm_conv,
                   out_norm_attn=out_norm_attn, w_out=w_out, ffn2_norm=ffn2_norm,
                   ffn2_w13=ffn2_w13, ffn2_w2=ffn2_w2)
    layer = {name: arr[0] for name, arr in stacked.items()}
    return _layer(x, layer, final_norm, plan)
```

```python
import functools

import numpy as np
import jax
import jax.numpy as jnp
from jax import lax
from jax.experimental import pallas as pl
from jax.experimental.pallas import tpu as pltpu

D_MODEL = 1024
D_CONV = 512
CONV_WIDTH = 31
N_HEADS = 8
HEAD_DIM = 64
D_ATTN = N_HEADS * HEAD_DIM
D_MIX = D_CONV + D_ATTN
D_FF = 2816
EPS = 1e-6
LOG2E = 1.4426950408889634

V7X_LANES = 128
V7X_SUBLANES = 8
BF16_SUBLANES = 16
V7X_MXU_DIM = 256
V7X_VMEM_BYTES = 64 * 1024 * 1024

PAIR = 2 * HEAD_DIM
N_PAIRS = N_HEADS // 2
F_PAD = V7X_LANES
N_IN_MAIN = 2 * D_CONV + 3 * D_ATTN
HALO = 32
CONV_ROWS = 16
LOAD_ROWS, LOAD_COLS = 256, 512
KEY_BLOCKS = 2
SCORE_AHEAD = 3
ONES_ROWS = 16

BF16 = jnp.bfloat16
F32 = jnp.float32


def _plan():
    def hidden_chunks(mxu_tiles):
        step = mxu_tiles * V7X_MXU_DIM
        return tuple((c, min(c + step, D_FF)) for c in range(0, D_FF, step))

    return dict(
        tm=512,
        tm_out=1024,
        tq=256,
        ffn_chunks=hidden_chunks(6),
        ffn_in_chunks=hidden_chunks(1),
        vmem_limit=V7X_VMEM_BYTES - 8 * 1024 * 1024,
    )


def _rms(x, g):
    return x * lax.rsqrt(jnp.mean(x * x, axis=-1, keepdims=True) + EPS) * g


def _split3(x):
    hi = x.astype(BF16)
    r = x - hi.astype(F32)
    mid = r.astype(BF16)
    lo = (r - mid.astype(F32)).astype(BF16)
    return hi, mid, lo


def _pack3(x):
    hi, mid, lo = _split3(x)
    packed = (hi.astype(F32) + pltpu.roll(mid.astype(F32), N_HEADS, axis=1)
              + pltpu.roll(lo.astype(F32), 2 * N_HEADS, axis=1))
    return packed.astype(BF16)


def _swiglu(h_bf16, w13_ref, w2_ref, chunks, side_work=None):
    acc = None
    zero = None
    for c, (c0, c1) in enumerate(chunks):
        gate = jnp.dot(h_bf16, w13_ref[:, c0:c1], preferred_element_type=F32)
        up = jnp.dot(h_bf16, w13_ref[:, D_FF + c0:D_FF + c1], preferred_element_type=F32)
        if zero is not None:
            gate = gate + zero
        zero = side_work[c]() if side_work is not None else None
        act = (gate * jax.nn.sigmoid(gate) * up).astype(BF16)
        part = jnp.dot(act, w2_ref[c0:c1, :], preferred_element_type=F32)
        acc = part if acc is None else acc + part
    return acc


def _zero_after(v):
    bits = pltpu.bitcast(v[0:V7X_SUBLANES, 0:V7X_LANES], jnp.uint32)
    bits = lax.shift_right_logical(lax.shift_right_logical(bits, jnp.uint32(16)), jnp.uint32(16))
    return pltpu.bitcast(bits, F32)[0:1, 0:1]


def _conv_rows(u_ref, s_ref, cw_ref, base):
    first = HALO - (CONV_WIDTH - 1)
    groups = CONV_ROWS // V7X_SUBLANES
    acc = None
    for kk in range(CONV_WIDTH):
        aligned, r = divmod(first + kk, V7X_SUBLANES)
        src = u_ref if r == 0 else s_ref.at[r - 1]
        lo = base + aligned * V7X_SUBLANES
        term = src[lo:lo + CONV_ROWS, :].reshape(groups, V7X_SUBLANES, D_CONV) * cw_ref[kk]
        acc = term if acc is None else acc + term
    return acc.reshape(CONV_ROWS, D_CONV)


def _load_as_bf16(jobs, stage_ref, sem):
    n_slots = stage_ref.shape[0]
    assert stage_ref.shape[1] >= max(LOAD_ROWS, LOAD_COLS) and stage_ref.shape[2] == LOAD_COLS
    assert all(rows % LOAD_ROWS == 0 and cols % LOAD_COLS == 0 for _, _, rows, cols, _ in jobs)
    tiles = [(src, dst, r0, c0, tr) for src, dst, rows, cols, tr in jobs
             for r0 in range(0, rows, LOAD_ROWS) for c0 in range(0, cols, LOAD_COLS)]

    def stage(t):
        transposed = tiles[t][4]
        shape = (LOAD_COLS, LOAD_ROWS) if transposed else (LOAD_ROWS, LOAD_COLS)
        return stage_ref.at[t % n_slots, 0:shape[0], 0:shape[1]]

    def copy(t):
        src, _, r0, c0, transposed = tiles[t]
        window = (src.at[pl.ds(c0, LOAD_COLS), pl.ds(r0, LOAD_ROWS)] if transposed
                  else src.at[pl.ds(r0, LOAD_ROWS), pl.ds(c0, LOAD_COLS)])
        return pltpu.make_async_copy(window, stage(t), sem.at[t % n_slots])

    for t in range(min(n_slots, len(tiles))):
        copy(t).start()
    for t, (_, dst, r0, c0, transposed) in enumerate(tiles):
        copy(t).wait()
        tile = stage(t)[...]
        dst[r0:r0 + LOAD_ROWS, c0:c0 + LOAD_COLS] = (tile.T if transposed else tile).astype(BF16)
        if t + n_slots < len(tiles):
            copy(t + n_slots).start()


def _ffn_in_body(x_ref, n1_ref, w13_hbm, w2_hbm, nm_ref, wint_hbm, wf_ref, fb_ref, place_ref,
                 cq_ref, ck_ref, cw_ref, cb_ref, lg_ref, lb_ref, on_ref,
                 x1_ref, yc_ref, qt_ref, k_ref, vt_ref, eqt_ref, ek_ref,
                 carry_ref, u_ref, s_ref, cacc_ref, w13_ref, w2_ref, win_ref, load_sem,
                 *, chunks, tm, tiles_per_seq):
    n = pl.program_id(0)

    @pl.when(n == 0)
    def _():
        _load_as_bf16([(w13_hbm, w13_ref, D_MODEL, 2 * D_FF, False),
                       (w2_hbm, w2_ref, D_FF, D_MODEL, False),
                       (wint_hbm, win_ref, D_MODEL, N_IN_MAIN, True)], s_ref, load_sem)
        win_ref[:, N_IN_MAIN:] = wf_ref[...].T.astype(BF16)
        u_ref[...] = jnp.zeros_like(u_ref)
        carry_ref[...] = jnp.zeros_like(carry_ref)

    def tile_sum(v):
        v = v.reshape(-1, V7X_SUBLANES, v.shape[-1]).sum(axis=0)
        return sum(v[:, g:g + V7X_LANES] for g in range(0, v.shape[-1], V7X_LANES))

    def shift_copies():
        for r in range(1, V7X_SUBLANES):
            s_ref[r - 1] = u_ref[r:r + s_ref.shape[1], :]
        return None

    def conv_group(bases):
        def run():
            seen = None
            for base in bases:
                rows = _conv_rows(u_ref, s_ref, cw_ref, base)
                cacc_ref[base:base + CONV_ROWS, :] = rows
                seen = tile_sum(rows) if seen is None else seen + tile_sum(rows)
            return _zero_after(seen)
        return run

    def conv_epilogue():
        cv = cacc_ref[...] + cb_ref[...]
        mu = jnp.mean(cv, axis=-1, keepdims=True)
        xc = cv - mu
        y = xc * lax.rsqrt(jnp.mean(xc * xc, axis=-1, keepdims=True) + EPS) * lg_ref[...] + lb_ref[...]
        y = y * jax.nn.sigmoid(y)
        y = _rms(y, on_ref[...])
        yc_ref[...] = y.astype(BF16)
        return _zero_after(tile_sum(y))

    bases = list(range(0, tm, CONV_ROWS))
    n_groups = len(chunks) - 3
    per_group = -(-len(bases) // n_groups)
    side_work = ([shift_copies]
                 + [conv_group(bases[g * per_group:(g + 1) * per_group]) for g in range(n_groups)]
                 + [conv_epilogue, lambda: None])

    flush_step = pl.num_programs(0) - 1

    @pl.when(n == flush_step)
    def _():
        for work in side_work:
            work()

    @pl.when(n < flush_step)
    def _():
        x = x_ref[...]
        h = _rms(x, n1_ref[...]).astype(BF16)
        x1 = x + 0.5 * _swiglu(h, w13_ref, w2_ref, chunks, side_work)
        x1_ref[...] = x1

        h2 = _rms(x1, nm_ref[...]).astype(BF16)
        proj = jnp.dot(h2, win_ref[...], preferred_element_type=F32)

        opens_sequence = lax.rem(n, tiles_per_seq) == 0
        u_ref[0:HALO, :] = jnp.where(opens_sequence, 0.0, u_ref[tm:tm + HALO, :])
        u_ref[HALO:HALO + tm, :] = proj[:, :D_CONV] * jax.nn.sigmoid(proj[:, D_CONV:2 * D_CONV])
        o = 2 * D_CONV
        qt_ref[...] = (proj[:, o:o + D_ATTN] * (HEAD_DIM ** -0.5 * LOG2E)).T.astype(BF16)
        k_ref[...] = proj[:, o + D_ATTN:o + 2 * D_ATTN].astype(BF16)
        vt_ref[...] = proj[:, o + 2 * D_ATTN:o + 3 * D_ATTN].T.astype(BF16)

        z = proj[:, N_IN_MAIN:] + fb_ref[...]
        logf = jnp.minimum(z, 0.0) - jnp.log1p(jnp.exp(-jnp.abs(z)))
        lane = lax.broadcasted_iota(jnp.int32, logf.shape, 1)
        logf = jnp.where(lane < N_HEADS, logf, 0.0)

        row = lax.broadcasted_iota(jnp.int32, (tm, tm), 0)
        col = lax.broadcasted_iota(jnp.int32, (tm, tm), 1)
        tril = jnp.where(row >= col, 1.0, 0.0).astype(BF16)
        d_packed = jnp.dot(tril, _pack3(logf), preferred_element_type=F32)
        d_loc = (d_packed + pltpu.roll(d_packed, F_PAD - N_HEADS, axis=1)
                 + pltpu.roll(d_packed, F_PAD - 2 * N_HEADS, axis=1))
        d_loc = jnp.where(lane < N_HEADS, d_loc, 0.0)

        d = d_loc + jnp.where(opens_sequence, 0.0, carry_ref[0:1, :])
        carry_ref[0:1, :] = d[tm - 1:tm, :]

        placed = jnp.dot(_pack3(d * LOG2E), place_ref[...], preferred_element_type=F32)
        eqt_ref[...] = (placed[:, :D_ATTN] + cq_ref[...]).T.astype(BF16)
        ek_ref[...] = (placed[:, D_ATTN:] + ck_ref[...]).astype(BF16)


def _attn_body(qt_ref, eqt_ref, k_ref, ek_ref, vt_ref, wa_ref, wb_ref, wc_ref,
               o_ref, wa_out, wb_out, wc_out, ka_ref, va_ref, *, seq, tq):
    for w_in_ref, w_out_ref in ((wa_ref, wa_out), (wb_ref, wb_out), (wc_ref, wc_out)):
        w_out_ref[...] = w_in_ref[...].astype(BF16)

    ka_ref[:, 0:PAIR] = k_ref[...]
    ka_ref[:, PAIR:2 * PAIR] = ek_ref[...]
    for hh in range(2):
        va_ref[hh, 0:HEAD_DIM, :] = vt_ref[hh * HEAD_DIM:(hh + 1) * HEAD_DIM, :]
        va_ref[hh, HEAD_DIM:HEAD_DIM + ONES_ROWS, :] = jnp.ones((ONES_ROWS, seq), BF16)

    key = lax.broadcasted_iota(jnp.int32, (tq, 2 * tq), 0)
    qry = lax.broadcasted_iota(jnp.int32, (tq, 2 * tq), 1)
    causal = key <= jnp.where(qry < tq, qry, qry - tq)
    zeros = jnp.zeros((HEAD_DIM, tq), BF16)

    def masked_queries(i):
        q, e = qt_ref[:, i * tq:(i + 1) * tq], eqt_ref[:, i * tq:(i + 1) * tq]
        h0 = jnp.concatenate([q[0:HEAD_DIM], zeros, e[0:HEAD_DIM], zeros], axis=0)
        h1 = jnp.concatenate([zeros, q[HEAD_DIM:PAIR], zeros, e[HEAD_DIM:PAIR]], axis=0)
        return jnp.concatenate([h0, h1], axis=1)

    qm = {}

    def scores(i, j, w):
        if i not in qm:
            qm.clear()
            qm[i] = masked_queries(i)
        s = jnp.dot(ka_ref[j * tq:(j + w) * tq, :], qm[i], preferred_element_type=F32)
        if j + w <= i:
            return s
        diag = jnp.where(causal, s[(w - 1) * tq:, :], -jnp.inf)
        return diag if w == 1 else jnp.concatenate([s[0:(w - 1) * tq, :], diag], axis=0)

    tasks = [(i, j, min(KEY_BLOCKS, i + 1 - j))
             for i in range(seq // tq) for j in range(0, i + 1, KEY_BLOCKS)]
    ahead = [scores(*t) for t in tasks[:SCORE_AHEAD]]
    m = acc = None
    for n, (i, j, w) in enumerate(tasks):
        s = ahead.pop(0)
        if n + SCORE_AHEAD < len(tasks):
            ahead.append(scores(*tasks[n + SCORE_AHEAD]))
        m_blk = jnp.max(s, axis=0, keepdims=True)
        m_new = m_blk if j == 0 else jnp.maximum(m, m_blk)
        p = jnp.exp2(s - m_new).astype(BF16)
        o = [jnp.dot(va_ref[hh, :, j * tq:(j + w) * tq], p[:, hh * tq:(hh + 1) * tq],
                     preferred_element_type=F32) for hh in range(2)]
        if j == 0:
            acc = o
        else:
            alpha = jnp.exp2(m - m_new)
            acc = [acc[hh] * alpha[:, hh * tq:(hh + 1) * tq] + o[hh] for hh in range(2)]
        m = m_new
        if j + w == i + 1:
            heads = [a[0:HEAD_DIM, :] * (1.0 / a[HEAD_DIM:HEAD_DIM + 1, :]) for a in acc]
            o_ref[i * tq:(i + 1) * tq, :] = jnp.concatenate(heads, axis=0).T


def _out_ffn_body(x1_ref, yc_ref, ya_ref, na_ref, wo_ref, n2_ref, w13_ref, w2_ref, nf_ref,
                  o_ref, *, chunks):
    ya = _rms(ya_ref[...], na_ref[...]).astype(BF16)
    y = jnp.dot(jnp.concatenate([yc_ref[...], ya], axis=1), wo_ref[...], preferred_element_type=F32)
    x2 = x1_ref[...] + y
    h = _rms(x2, n2_ref[...]).astype(BF16)
    x3 = x2 + 0.5 * _swiglu(h, w13_ref, w2_ref, chunks)
    o_ref[...] = _rms(x3, nf_ref[...])


def _resident(shape):
    nd = len(shape)
    return pl.BlockSpec(shape, lambda *_: (0,) * nd, pipeline_mode=pl.Buffered(1))


def _placement():
    place = np.zeros((F_PAD, 2 * D_ATTN), np.float32)
    cq = np.zeros((1, D_ATTN), np.float32)
    ck = np.zeros((1, D_ATTN), np.float32)
    for hd in range(N_HEADS):
        base = hd * HEAD_DIM
        for i in range(3):
            place[i * N_HEADS + hd, base + i] = 1.0
            place[i * N_HEADS + hd, D_ATTN + base + 3 + i] = -1.0
            cq[0, base + 3 + i] = 1.0
            ck[0, base + i] = 1.0
    return jnp.asarray(place, BF16), jnp.asarray(cq), jnp.asarray(ck)


def _layer(x, p, final_norm, plan):
    bsz, seq, _ = x.shape
    tm, tq = plan["tm"], plan["tq"]
    chunks = plan["ffn_chunks"]
    cparams = functools.partial(pltpu.CompilerParams, vmem_limit_bytes=plan["vmem_limit"])
    row2 = lambda v: v.reshape(1, -1)

    w_in_t = jnp.swapaxes(p["w_in"], 0, 1)
    w_f_t = jnp.pad(w_in_t[N_IN_MAIN:], ((0, F_PAD - N_HEADS), (0, 0)))
    in_hbm = pl.BlockSpec(memory_space=pl.ANY)
    fb = jnp.pad(p["forget_b"], (0, F_PAD - N_HEADS)).reshape(1, F_PAD)
    place, cq, ck = _placement()

    tiles_per_seq = seq // tm
    n_tiles = bsz * tiles_per_seq
    cur = lambda n: jnp.minimum(n, n_tiles - 1)
    flat = lambda width: pl.BlockSpec((tm, width), lambda n: (cur(n), 0))
    flat_t = pl.BlockSpec((None, D_ATTN, tm),
                          lambda n: (cur(n) // tiles_per_seq, 0, cur(n) % tiles_per_seq))
    lagged = pl.BlockSpec((tm, D_CONV), lambda n: (jnp.maximum(n - 1, 0), 0))
    attn_t = jax.ShapeDtypeStruct((bsz, D_ATTN, seq), BF16)
    attn_n = jax.ShapeDtypeStruct((bsz * seq, D_ATTN), BF16)
    x1, yc, qt, k, vt, eqt, ek = pl.pallas_call(
        functools.partial(_ffn_in_body, chunks=plan["ffn_in_chunks"], tm=tm,
                          tiles_per_seq=tiles_per_seq),
        grid=(n_tiles + 1,),
        in_specs=[flat(D_MODEL), _resident((1, D_MODEL)), in_hbm, in_hbm, _resident((1, D_MODEL)),
                  in_hbm, _resident((F_PAD, D_MODEL)),
                  _resident((1, F_PAD)), _resident((F_PAD, 2 * D_ATTN)),
                  _resident((1, D_ATTN)), _resident((1, D_ATTN)),
                  _resident((CONV_WIDTH, V7X_SUBLANES, D_CONV))] + [_resident((1, D_CONV))] * 4,
        out_specs=[flat(D_MODEL), lagged, flat_t, flat(D_ATTN), flat_t, flat_t, flat(D_ATTN)],
        out_shape=[jax.ShapeDtypeStruct((bsz * seq, D_MODEL), F32),
                   jax.ShapeDtypeStruct((bsz * seq, D_CONV), BF16),
                   attn_t, attn_n, attn_t, attn_t, attn_n],
        scratch_shapes=[pltpu.VMEM((V7X_SUBLANES, F_PAD), F32),
                        pltpu.VMEM((HALO + tm, D_CONV), F32),
                        pltpu.VMEM((V7X_SUBLANES - 1, HALO + tm - V7X_SUBLANES, D_CONV), F32),
                        pltpu.VMEM((tm, D_CONV), F32),
                        pltpu.VMEM((D_MODEL, 2 * D_FF), BF16),
                        pltpu.VMEM((D_FF, D_MODEL), BF16),
                        pltpu.VMEM((D_MODEL, N_IN_MAIN + F_PAD), BF16),
                        pltpu.SemaphoreType.DMA((V7X_SUBLANES - 1,))],
        compiler_params=cparams(dimension_semantics=("arbitrary",)),
        name="ffn_in",
    )(x.reshape(bsz * seq, D_MODEL), row2(p["ffn1_norm"]), p["ffn1_w13"], p["ffn1_w2"],
      row2(p["mix_norm"]), w_in_t, w_f_t, fb, place, cq, ck,
      jnp.broadcast_to(p["conv_w"][:, None, :], (CONV_WIDTH, V7X_SUBLANES, D_CONV)),
      row2(p["conv_b"]), row2(p["conv_ln_g"]), row2(p["conv_ln_b"]), row2(p["out_norm_conv"]))
    x1 = x1.reshape(bsz, seq, D_MODEL)
    yc = yc.reshape(bsz, seq, D_CONV)
    k = k.reshape(bsz, seq, D_ATTN)
    ek = ek.reshape(bsz, seq, D_ATTN)

    pair = pl.BlockSpec((None, seq, PAIR), lambda b, g: (b, 0, g))
    pair_t = pl.BlockSpec((None, PAIR, seq), lambda b, g: (b, g, 0))
    steps = bsz * N_PAIRS

    def n_slabs(w):
        return max(d for d in range(1, steps + 1)
                   if steps % d == 0 and w.shape[0] % (d * BF16_SUBLANES) == 0)

    slab = lambda w: w.reshape(n_slabs(w), w.shape[0] // n_slabs(w), w.shape[1])
    slab_spec = lambda w: pl.BlockSpec(
        (None,) + slab(w).shape[1:],
        lambda b, g, per=steps // n_slabs(w): ((b * N_PAIRS + g) // per, 0, 0))
    late_w = [p["w_out"], p["ffn2_w13"], p["ffn2_w2"]]
    ya, w_out, w13_2, w2_2 = pl.pallas_call(
        functools.partial(_attn_body, seq=seq, tq=tq),
        grid=(bsz, N_PAIRS),
        in_specs=[pair_t, pair_t, pair, pair, pair_t] + [slab_spec(w) for w in late_w],
        out_specs=[pair] + [slab_spec(w) for w in late_w],
        out_shape=[jax.ShapeDtypeStruct((bsz, seq, D_ATTN), F32)]
                  + [jax.ShapeDtypeStruct(slab(w).shape, BF16) for w in late_w],
        scratch_shapes=[pltpu.VMEM((seq, 2 * PAIR), BF16),
                        pltpu.VMEM((2, HEAD_DIM + ONES_ROWS, seq), BF16)],
        compiler_params=cparams(dimension_semantics=("arbitrary", "arbitrary")),
        name="attn",
    )(qt, eqt, k, ek, vt, *[slab(w) for w in late_w])
    w_out, w13_2, w2_2 = [w.reshape(src.shape) for w, src in zip((w_out, w13_2, w2_2), late_w)]

    tmo = plan["tm_out"]
    tok = lambda width: pl.BlockSpec((None, tmo, width), lambda b, s: (b, s, 0))
    out = pl.pallas_call(
        functools.partial(_out_ffn_body, chunks=chunks),
        grid=(bsz, seq // tmo),
        in_specs=[tok(D_MODEL), tok(D_CONV), tok(D_ATTN), _resident((1, D_ATTN)),
                  _resident((D_MIX, D_MODEL)), _resident((1, D_MODEL)),
                  _resident((D_MODEL, 2 * D_FF)), _resident((D_FF, D_MODEL)),
                  _resident((1, D_MODEL))],
        out_specs=tok(D_MODEL),
        out_shape=jax.ShapeDtypeStruct((bsz, seq, D_MODEL), F32),
        compiler_params=cparams(dimension_semantics=("arbitrary", "arbitrary")),
        name="out_ffn",
    )(x1, yc, ya, row2(p["out_norm_attn"]), w_out, row2(p["ffn2_norm"]), w13_2, w2_2,
      row2(final_norm))
    return out


def kernel(x, ffn1_norm, ffn1_w13, ffn1_w2, mix_norm, w_in, conv_w, conv_b, conv_ln_g, conv_ln_b,
           forget_b, out_norm_conv, out_norm_attn, w_out, ffn2_norm, ffn2_w13, ffn2_w2, final_norm):
    depth = ffn1_norm.shape[0]
    assert depth == 1, "the fused final RMSNorm assumes a single layer"
    plan = _plan()
    stacked = dict(ffn1_norm=ffn1_norm, ffn1_w13=ffn1_w13, ffn1_w2=ffn1_w2, mix_norm=mix_norm,
                   w_in=w_in, conv_w=conv_w, conv_b=conv_b, conv_ln_g=conv_ln_g,
                   conv_ln_b=conv_ln_b, forget_b=forget_b, out_norm_conv=out_norm_conv,
                   out_norm_attn=out_norm_attn, w_out=w_out, ffn2_norm=ffn2_norm,
                   ffn2_w13=ffn2_w13, ffn2_w2=ffn2_w2)
    layer = {name: arr[0] for name, arr in stacked.items()}
    return _layer(x, layer, final_norm, plan)
```

```python
import functools

import numpy as np
import jax
import jax.numpy as jnp
from jax import lax
from jax.experimental import pallas as pl
from jax.experimental.pallas import tpu as pltpu

D_MODEL = 1024
D_CONV = 512
CONV_WIDTH = 31
N_HEADS = 8
HEAD_DIM = 64
D_ATTN = N_HEADS * HEAD_DIM
D_MIX = D_CONV + D_ATTN
D_FF = 2816
EPS = 1e-6
LOG2E = 1.4426950408889634

V7X_LANES = 128
V7X_SUBLANES = 8
BF16_SUBLANES = 16
V7X_MXU_DIM = 256
V7X_VMEM_BYTES = 64 * 1024 * 1024

PAIR = 2 * HEAD_DIM
N_PAIRS = N_HEADS // 2
F_PAD = V7X_LANES
N_IN_MAIN = 2 * D_CONV + 3 * D_ATTN
HALO = 32
CONV_ROWS = 16
LOAD_ROWS, LOAD_COLS = 256, 512
KEY_BLOCKS = 2
ONES_ROWS = 16

BF16 = jnp.bfloat16
F32 = jnp.float32


def _plan():
    def hidden_chunks(mxu_tiles):
        step = mxu_tiles * V7X_MXU_DIM
        return tuple((c, min(c + step, D_FF)) for c in range(0, D_FF, step))

    return dict(
        tm=512,
        tm_out=1024,
        tq=256,
        ffn_chunks=hidden_chunks(6),
        ffn_in_chunks=hidden_chunks(1),
        vmem_limit=V7X_VMEM_BYTES - 8 * 1024 * 1024,
    )


def _rms(x, g):
    return x * lax.rsqrt(jnp.mean(x * x, axis=-1, keepdims=True) + EPS) * g


def _split3(x):
    hi = x.astype(BF16)
    r = x - hi.astype(F32)
    mid = r.astype(BF16)
    lo = (r - mid.astype(F32)).astype(BF16)
    return hi, mid, lo


def _pack3(x):
    hi, mid, lo = _split3(x)
    packed = (hi.astype(F32) + pltpu.roll(mid.astype(F32), N_HEADS, axis=1)
              + pltpu.roll(lo.astype(F32), 2 * N_HEADS, axis=1))
    return packed.astype(BF16)


def _swiglu(h_bf16, w13_ref, w2_ref, chunks, side_work=None):
    acc = None
    zero = None
    for c, (c0, c1) in enumerate(chunks):
        gate = jnp.dot(h_bf16, w13_ref[:, c0:c1], preferred_element_type=F32)
        up = jnp.dot(h_bf16, w13_ref[:, D_FF + c0:D_FF + c1], preferred_element_type=F32)
        if zero is not None:
            gate = gate + zero
        zero = side_work[c]() if side_work is not None else None
        act = (gate * jax.nn.sigmoid(gate) * up).astype(BF16)
        part = jnp.dot(act, w2_ref[c0:c1, :], preferred_element_type=F32)
        acc = part if acc is None else acc + part
    return acc


def _zero_after(v):
    bits = pltpu.bitcast(v[0:V7X_SUBLANES, 0:V7X_LANES], jnp.uint32)
    bits = lax.shift_right_logical(lax.shift_right_logical(bits, jnp.uint32(16)), jnp.uint32(16))
    return pltpu.bitcast(bits, F32)[0:1, 0:1]


def _conv_rows(u_ref, s_ref, cw_ref, base):
    first = HALO - (CONV_WIDTH - 1)
    groups = CONV_ROWS // V7X_SUBLANES
    acc = None
    for kk in range(CONV_WIDTH):
        aligned, r = divmod(first + kk, V7X_SUBLANES)
        src = u_ref if r == 0 else s_ref.at[r - 1]
        lo = base + aligned * V7X_SUBLANES
        term = src[lo:lo + CONV_ROWS, :].reshape(groups, V7X_SUBLANES, D_CONV) * cw_ref[kk]
        acc = term if acc is None else acc + term
    return acc.reshape(CONV_ROWS, D_CONV)


def _load_as_bf16(jobs, stage_ref, sem):
    n_slots = stage_ref.shape[0]
    assert stage_ref.shape[1] >= max(LOAD_ROWS, LOAD_COLS) and stage_ref.shape[2] == LOAD_COLS
    assert all(rows % LOAD_ROWS == 0 and cols % LOAD_COLS == 0 for _, _, rows, cols, _ in jobs)
    tiles = [(src, dst, r0, c0, tr) for src, dst, rows, cols, tr in jobs
             for r0 in range(0, rows, LOAD_ROWS) for c0 in range(0, cols, LOAD_COLS)]

    def stage(t):
        transposed = tiles[t][4]
        shape = (LOAD_COLS, LOAD_ROWS) if transposed else (LOAD_ROWS, LOAD_COLS)
        return stage_ref.at[t % n_slots, 0:shape[0], 0:shape[1]]

    def copy(t):
        src, _, r0, c0, transposed = tiles[t]
        window = (src.at[pl.ds(c0, LOAD_COLS), pl.ds(r0, LOAD_ROWS)] if transposed
                  else src.at[pl.ds(r0, LOAD_ROWS), pl.ds(c0, LOAD_COLS)])
        return pltpu.make_async_copy(window, stage(t), sem.at[t % n_slots])

    for t in range(min(n_slots, len(tiles))):
        copy(t).start()
    for t, (_, dst, r0, c0, transposed) in enumerate(tiles):
        copy(t).wait()
        tile = stage(t)[...]
        dst[r0:r0 + LOAD_ROWS, c0:c0 + LOAD_COLS] = (tile.T if transposed else tile).astype(BF16)
        if t + n_slots < len(tiles):
            copy(t + n_slots).start()


def _ffn_in_body(x_ref, n1_ref, w13_hbm, w2_hbm, nm_ref, wint_hbm, wf_ref, fb_ref, place_ref,
                 cq_ref, ck_ref, cw_ref, cb_ref, lg_ref, lb_ref, on_ref,
                 x1_ref, yc_ref, qt_ref, k_ref, vt_ref, eqt_ref, ek_ref,
                 carry_ref, u_ref, s_ref, cacc_ref, w13_ref, w2_ref, win_ref, load_sem,
                 *, chunks, tm, tiles_per_seq):
    n = pl.program_id(0)

    @pl.when(n == 0)
    def _():
        _load_as_bf16([(w13_hbm, w13_ref, D_MODEL, 2 * D_FF, False),
                       (w2_hbm, w2_ref, D_FF, D_MODEL, False),
                       (wint_hbm, win_ref, D_MODEL, N_IN_MAIN, True)], s_ref, load_sem)
        win_ref[:, N_IN_MAIN:] = wf_ref[...].T.astype(BF16)
        u_ref[...] = jnp.zeros_like(u_ref)
        carry_ref[...] = jnp.zeros_like(carry_ref)

    def tile_sum(v):
        v = v.reshape(-1, V7X_SUBLANES, v.shape[-1]).sum(axis=0)
        return sum(v[:, g:g + V7X_LANES] for g in range(0, v.shape[-1], V7X_LANES))

    def shift_copies():
        for r in range(1, V7X_SUBLANES):
            s_ref[r - 1] = u_ref[r:r + s_ref.shape[1], :]
        return None

    def conv_group(bases):
        def run():
            seen = None
            for base in bases:
                rows = _conv_rows(u_ref, s_ref, cw_ref, base)
                cacc_ref[base:base + CONV_ROWS, :] = rows
                seen = tile_sum(rows) if seen is None else seen + tile_sum(rows)
            return _zero_after(seen)
        return run

    def conv_epilogue():
        cv = cacc_ref[...] + cb_ref[...]
        mu = jnp.mean(cv, axis=-1, keepdims=True)
        xc = cv - mu
        y = xc * lax.rsqrt(jnp.mean(xc * xc, axis=-1, keepdims=True) + EPS) * lg_ref[...] + lb_ref[...]
        y = y * jax.nn.sigmoid(y)
        y = _rms(y, on_ref[...])
        yc_ref[...] = y.astype(BF16)
        return _zero_after(tile_sum(y))

    bases = list(range(0, tm, CONV_ROWS))
    n_groups = len(chunks) - 2
    side_work = ([shift_copies]
                 + [conv_group(bases[g * len(bases) // n_groups:(g + 1) * len(bases) // n_groups])
                    for g in range(n_groups)]
                 + [conv_epilogue])

    flush_step = pl.num_programs(0) - 1

    @pl.when(n == flush_step)
    def _():
        for work in side_work:
            work()

    @pl.when(n < flush_step)
    def _():
        x = x_ref[...]
        h = _rms(x, n1_ref[...]).astype(BF16)
        x1 = x + 0.5 * _swiglu(h, w13_ref, w2_ref, chunks, side_work)
        x1_ref[...] = x1

        h2 = _rms(x1, nm_ref[...]).astype(BF16)
        proj = jnp.dot(h2, win_ref[...], preferred_element_type=F32)

        opens_sequence = lax.rem(n, tiles_per_seq) == 0
        u_ref[0:HALO, :] = jnp.where(opens_sequence, 0.0, u_ref[tm:tm + HALO, :])
        u_ref[HALO:HALO + tm, :] = proj[:, :D_CONV] * jax.nn.sigmoid(proj[:, D_CONV:2 * D_CONV])
        o = 2 * D_CONV
        qt_ref[...] = (proj[:, o:o + D_ATTN] * (HEAD_DIM ** -0.5 * LOG2E)).T.astype(BF16)
        k_ref[...] = proj[:, o + D_ATTN:o + 2 * D_ATTN].astype(BF16)
        vt_ref[...] = proj[:, o + 2 * D_ATTN:o + 3 * D_ATTN].T.astype(BF16)

        z = proj[:, N_IN_MAIN:] + fb_ref[...]
        logf = jnp.minimum(z, 0.0) - jnp.log1p(jnp.exp(-jnp.abs(z)))
        lane = lax.broadcasted_iota(jnp.int32, logf.shape, 1)
        logf = jnp.where(lane < N_HEADS, logf, 0.0)

        row = lax.broadcasted_iota(jnp.int32, (tm, tm), 0)
        col = lax.broadcasted_iota(jnp.int32, (tm, tm), 1)
        tril = jnp.where(row >= col, 1.0, 0.0).astype(BF16)
        d_packed = jnp.dot(tril, _pack3(logf), preferred_element_type=F32)
        d_loc = (d_packed + pltpu.roll(d_packed, F_PAD - N_HEADS, axis=1)
                 + pltpu.roll(d_packed, F_PAD - 2 * N_HEADS, axis=1))
        d_loc = jnp.where(lane < N_HEADS, d_loc, 0.0)

        d = d_loc + jnp.where(opens_sequence, 0.0, carry_ref[0:1, :])
        carry_ref[0:1, :] = d[tm - 1:tm, :]

        placed = jnp.dot(_pack3(d * LOG2E), place_ref[...], preferred_element_type=F32)
        eqt_ref[...] = (placed[:, :D_ATTN] + cq_ref[...]).T.astype(BF16)
        ek_ref[...] = (placed[:, D_ATTN:] + ck_ref[...]).astype(BF16)


def _attn_body(qt_ref, eqt_ref, k_ref, ek_ref, vt_ref, wa_ref, wb_ref, wc_ref,
               o_ref, wa_out, wb_out, wc_out, ka_ref, va_ref, *, seq, tq):
    for w_in_ref, w_out_ref in ((wa_ref, wa_out), (wb_ref, wb_out), (wc_ref, wc_out)):
        w_out_ref[...] = w_in_ref[...].astype(BF16)

    ka_ref[:, 0:PAIR] = k_ref[...]
    ka_ref[:, PAIR:2 * PAIR] = ek_ref[...]
    for hh in range(2):
        va_ref[hh, 0:HEAD_DIM, :] = vt_ref[hh * HEAD_DIM:(hh + 1) * HEAD_DIM, :]
        va_ref[hh, HEAD_DIM:HEAD_DIM + ONES_ROWS, :] = jnp.ones((ONES_ROWS, seq), BF16)

    key = lax.broadcasted_iota(jnp.int32, (tq, 2 * tq), 0)
    qry = lax.broadcasted_iota(jnp.int32, (tq, 2 * tq), 1)
    causal = key <= jnp.where(qry < tq, qry, qry - tq)
    zeros = jnp.zeros((HEAD_DIM, tq), BF16)

    def masked_queries(i):
        q, e = qt_ref[:, i * tq:(i + 1) * tq], eqt_ref[:, i * tq:(i + 1) * tq]
        h0 = jnp.concatenate([q[0:HEAD_DIM], zeros, e[0:HEAD_DIM], zeros], axis=0)
        h1 = jnp.concatenate([zeros, q[HEAD_DIM:PAIR], zeros, e[HEAD_DIM:PAIR]], axis=0)
        return jnp.concatenate([h0, h1], axis=1)

    qm = {}

    def scores(i, j, w):
        if i not in qm:
            qm.clear()
            qm[i] = masked_queries(i)
        s = jnp.dot(ka_ref[j * tq:(j + w) * tq, :], qm[i], preferred_element_type=F32)
        if j + w <= i:
            return s
        diag = jnp.where(causal, s[(w - 1) * tq:, :], -jnp.inf)
        return diag if w == 1 else jnp.concatenate([s[0:(w - 1) * tq, :], diag], axis=0)

    tasks = [(i, j, min(KEY_BLOCKS, i + 1 - j))
             for i in range(seq // tq) for j in range(0, i + 1, KEY_BLOCKS)]
    s_next = scores(*tasks[0])
    m = acc = None
    for n, (i, j, w) in enumerate(tasks):
        s = s_next
        if n + 1 < len(tasks):
            s_next = scores(*tasks[n + 1])
        m_blk = jnp.max(s, axis=0, keepdims=True)
        m_new = m_blk if j == 0 else jnp.maximum(m, m_blk)
        p = jnp.exp2(s - m_new).astype(BF16)
        o = [jnp.dot(va_ref[hh, :, j * tq:(j + w) * tq], p[:, hh * tq:(hh + 1) * tq],
                     preferred_element_type=F32) for hh in range(2)]
        if j == 0:
            acc = o
        else:
            alpha = jnp.exp2(m - m_new)
            acc = [acc[hh] * alpha[:, hh * tq:(hh + 1) * tq] + o[hh] for hh in range(2)]
        m = m_new
        if j + w == i + 1:
            heads = [a[0:HEAD_DIM, :] * (1.0 / a[HEAD_DIM:HEAD_DIM + 1, :]) for a in acc]
            o_ref[i * tq:(i + 1) * tq, :] = jnp.concatenate(heads, axis=0).T


def _out_ffn_body(x1_ref, yc_ref, ya_ref, na_ref, wo_ref, n2_ref, w13_ref, w2_ref, nf_ref,
                  o_ref, *, chunks):
    ya = _rms(ya_ref[...], na_ref[...]).astype(BF16)
    y = jnp.dot(jnp.concatenate([yc_ref[...], ya], axis=1), wo_ref[...], preferred_element_type=F32)
    x2 = x1_ref[...] + y
    h = _rms(x2, n2_ref[...]).astype(BF16)
    x3 = x2 + 0.5 * _swiglu(h, w13_ref, w2_ref, chunks)
    o_ref[...] = _rms(x3, nf_ref[...])


def _resident(shape):
    nd = len(shape)
    return pl.BlockSpec(shape, lambda *_: (0,) * nd, pipeline_mode=pl.Buffered(1))


def _placement():
    place = np.zeros((F_PAD, 2 * D_ATTN), np.float32)
    cq = np.zeros((1, D_ATTN), np.float32)
    ck = np.zeros((1, D_ATTN), np.float32)
    for hd in range(N_HEADS):
        base = hd * HEAD_DIM
        for i in range(3):
            place[i * N_HEADS + hd, base + i] = 1.0
            place[i * N_HEADS + hd, D_ATTN + base + 3 + i] = -1.0
            cq[0, base + 3 + i] = 1.0
            ck[0, base + i] = 1.0
    return jnp.asarray(place, BF16), jnp.asarray(cq), jnp.asarray(ck)


def _layer(x, p, final_norm, plan):
    bsz, seq, _ = x.shape
    tm, tq = plan["tm"], plan["tq"]
    chunks = plan["ffn_chunks"]
    cparams = functools.partial(pltpu.CompilerParams, vmem_limit_bytes=plan["vmem_limit"])
    row2 = lambda v: v.reshape(1, -1)

    w_in_t = jnp.swapaxes(p["w_in"], 0, 1)
    w_f_t = jnp.pad(w_in_t[N_IN_MAIN:], ((0, F_PAD - N_HEADS), (0, 0)))
    in_hbm = pl.BlockSpec(memory_space=pl.ANY)
    fb = jnp.pad(p["forget_b"], (0, F_PAD - N_HEADS)).reshape(1, F_PAD)
    place, cq, ck = _placement()

    tiles_per_seq = seq // tm
    n_tiles = bsz * tiles_per_seq
    cur = lambda n: jnp.minimum(n, n_tiles - 1)
    flat = lambda width: pl.BlockSpec((tm, width), lambda n: (cur(n), 0))
    flat_t = pl.BlockSpec((None, D_ATTN, tm),
                          lambda n: (cur(n) // tiles_per_seq, 0, cur(n) % tiles_per_seq))
    lagged = pl.BlockSpec((tm, D_CONV), lambda n: (jnp.maximum(n - 1, 0), 0))
    attn_t = jax.ShapeDtypeStruct((bsz, D_ATTN, seq), BF16)
    attn_n = jax.ShapeDtypeStruct((bsz * seq, D_ATTN), BF16)
    x1, yc, qt, k, vt, eqt, ek = pl.pallas_call(
        functools.partial(_ffn_in_body, chunks=plan["ffn_in_chunks"], tm=tm,
                          tiles_per_seq=tiles_per_seq),
        grid=(n_tiles + 1,),
        in_specs=[flat(D_MODEL), _resident((1, D_MODEL)), in_hbm, in_hbm, _resident((1, D_MODEL)),
                  in_hbm, _resident((F_PAD, D_MODEL)),
                  _resident((1, F_PAD)), _resident((F_PAD, 2 * D_ATTN)),
                  _resident((1, D_ATTN)), _resident((1, D_ATTN)),
                  _resident((CONV_WIDTH, V7X_SUBLANES, D_CONV))] + [_resident((1, D_CONV))] * 4,
        out_specs=[flat(D_MODEL), lagged, flat_t, flat(D_ATTN), flat_t, flat_t, flat(D_ATTN)],
        out_shape=[jax.ShapeDtypeStruct((bsz * seq, D_MODEL), F32),
                   jax.ShapeDtypeStruct((bsz * seq, D_CONV), BF16),
                   attn_t, attn_n, attn_t, attn_t, attn_n],
        scratch_shapes=[pltpu.VMEM((V7X_SUBLANES, F_PAD), F32),
                        pltpu.VMEM((HALO + tm, D_CONV), F32),
                        pltpu.VMEM((V7X_SUBLANES - 1, HALO + tm - V7X_SUBLANES, D_CONV), F32),
                        pltpu.VMEM((tm, D_CONV), F32),
                        pltpu.VMEM((D_MODEL, 2 * D_FF), BF16),
                        pltpu.VMEM((D_FF, D_MODEL), BF16),
                        pltpu.VMEM((D_MODEL, N_IN_MAIN + F_PAD), BF16),
                        pltpu.SemaphoreType.DMA((V7X_SUBLANES - 1,))],
        compiler_params=cparams(dimension_semantics=("arbitrary",)),
        name="ffn_in",
    )(x.reshape(bsz * seq, D_MODEL), row2(p["ffn1_norm"]), p["ffn1_w13"], p["ffn1_w2"],
      row2(p["mix_norm"]), w_in_t, w_f_t, fb, place, cq, ck,
      jnp.broadcast_to(p["conv_w"][:, None, :], (CONV_WIDTH, V7X_SUBLANES, D_CONV)),
      row2(p["conv_b"]), row2(p["conv_ln_g"]), row2(p["conv_ln_b"]), row2(p["out_norm_conv"]))
    x1 = x1.reshape(bsz, seq, D_MODEL)
    yc = yc.reshape(bsz, seq, D_CONV)
    k = k.reshape(bsz, seq, D_ATTN)
    ek = ek.reshape(bsz, seq, D_ATTN)

    pair = pl.BlockSpec((None, seq, PAIR), lambda b, g: (b, 0, g))
    pair_t = pl.BlockSpec((None, PAIR, seq), lambda b, g: (b, g, 0))
    steps = bsz * N_PAIRS

    def n_slabs(w):
        return max(d for d in range(1, steps + 1)
                   if steps % d == 0 and w.shape[0] % (d * BF16_SUBLANES) == 0)

    slab = lambda w: w.reshape(n_slabs(w), w.shape[0] // n_slabs(w), w.shape[1])
    slab_spec = lambda w: pl.BlockSpec(
        (None,) + slab(w).shape[1:],
        lambda b, g, per=steps // n_slabs(w): ((b * N_PAIRS + g) // per, 0, 0))
    late_w = [p["w_out"], p["ffn2_w13"], p["ffn2_w2"]]
    ya, w_out, w13_2, w2_2 = pl.pallas_call(
        functools.partial(_attn_body, seq=seq, tq=tq),
        grid=(bsz, N_PAIRS),
        in_specs=[pair_t, pair_t, pair, pair, pair_t] + [slab_spec(w) for w in late_w],
        out_specs=[pair] + [slab_spec(w) for w in late_w],
        out_shape=[jax.ShapeDtypeStruct((bsz, seq, D_ATTN), F32)]
                  + [jax.ShapeDtypeStruct(slab(w).shape, BF16) for w in late_w],
        scratch_shapes=[pltpu.VMEM((seq, 2 * PAIR), BF16),
                        pltpu.VMEM((2, HEAD_DIM + ONES_ROWS, seq), BF16)],
        compiler_params=cparams(dimension_semantics=("arbitrary", "arbitrary")),
        name="attn",
    )(qt, eqt, k, ek, vt, *[slab(w) for w in late_w])
    w_out, w13_2, w2_2 = [w.reshape(src.shape) for w, src in zip((w_out, w13_2, w2_2), late_w)]

    tmo = plan["tm_out"]
    tok = lambda width: pl.BlockSpec((None, tmo, width), lambda b, s: (b, s, 0))
    out = pl.pallas_call(
        functools.partial(_out_ffn_body, chunks=chunks),
        grid=(bsz, seq // tmo),
        in_specs=[tok(D_MODEL), tok(D_CONV), tok(D_ATTN), _resident((1, D_ATTN)),
                  _resident((D_MIX, D_MODEL)), _resident((1, D_MODEL)),
                  _resident((D_MODEL, 2 * D_FF)), _resident((D_FF, D_MODEL)),
                  _resident((1, D_MODEL))],
        out_specs=tok(D_MODEL),
        out_shape=jax.ShapeDtypeStruct((bsz, seq, D_MODEL), F32),
        compiler_params=cparams(dimension_semantics=("arbitrary", "arbitrary")),
        name="out_ffn",
    )(x1, yc, ya, row2(p["out_norm_attn"]), w_out, row2(p["ffn2_norm"]), w13_2, w2_2,
      row2(final_norm))
    return out


def kernel(x, ffn1_norm, ffn1_w13, ffn1_w2, mix_norm, w_in, conv_w, conv_b, conv_ln_g, conv_ln_b,
           forget_b, out_norm_conv, out_norm_attn, w_out, ffn2_norm, ffn2_w13, ffn2_w2, final_norm):
    depth = ffn1_norm.shape[0]
    assert depth == 1, "the fused final RMSNorm assumes a single layer"
    plan = _plan()
    stacked = dict(ffn1_norm=ffn1_norm, ffn1_w13=ffn1_w13, ffn1_w2=ffn1_w2, mix_norm=mix_norm,
                   w_in=w_in, conv_w=conv_w, conv_b=conv_b, conv_ln_g=conv_ln_g,
                   conv_ln_b=conv_ln_b, forget_b=forget_b, out_norm_conv=out_norm_conv,
                   out_norm_attn=out_norm_attn, w_out=w_out, ffn2_norm=ffn2_norm,
                   ffn2_w13=ffn2_w13, ffn2_w2=ffn2_w2)
    layer = {name: arr[0] for name, arr in stacked.items()}
    return _layer(x, layer, final_norm, plan)
```

```python
import functools

import numpy as np
import jax
import jax.numpy as jnp
from jax import lax
from jax.experimental import pallas as pl
from jax.experimental.pallas import tpu as pltpu

D_MODEL = 1024
D_CONV = 512
CONV_WIDTH = 31
N_HEADS = 8
HEAD_DIM = 64
D_ATTN = N_HEADS * HEAD_DIM
D_MIX = D_CONV + D_ATTN
D_FF = 2816
EPS = 1e-6
LOG2E = 1.4426950408889634

V7X_LANES = 128
V7X_SUBLANES = 8
BF16_SUBLANES = 16
V7X_MXU_DIM = 256
V7X_VMEM_BYTES = 64 * 1024 * 1024

PAIR = 2 * HEAD_DIM
N_PAIRS = N_HEADS // 2
F_PAD = V7X_LANES
N_IN_MAIN = 2 * D_CONV + 3 * D_ATTN
HALO = 32
CONV_ROWS = 16
LOAD_ROWS, LOAD_COLS = 256, 512
KEY_BLOCKS = 2
ONES_ROWS = 16

BF16 = jnp.bfloat16
F32 = jnp.float32


def _plan():
    def hidden_chunks(mxu_tiles):
        step = mxu_tiles * V7X_MXU_DIM
        return tuple((c, min(c + step, D_FF)) for c in range(0, D_FF, step))

    return dict(
        tm=512,
        tm_out=1024,
        tq=256,
        ffn_chunks=hidden_chunks(6),
        ffn_in_chunks=hidden_chunks(1),
        vmem_limit=V7X_VMEM_BYTES - 8 * 1024 * 1024,
    )


def _rms(x, g):
    return x * lax.rsqrt(jnp.mean(x * x, axis=-1, keepdims=True) + EPS) * g


def _split3(x):
    hi = x.astype(BF16)
    r = x - hi.astype(F32)
    mid = r.astype(BF16)
    lo = (r - mid.astype(F32)).astype(BF16)
    return hi, mid, lo


def _pack3(x):
    hi, mid, lo = _split3(x)
    packed = (hi.astype(F32) + pltpu.roll(mid.astype(F32), N_HEADS, axis=1)
              + pltpu.roll(lo.astype(F32), 2 * N_HEADS, axis=1))
    return packed.astype(BF16)


def _swiglu(h_bf16, w13_ref, w2_ref, chunks, side_work=None):
    acc = None
    zero = None
    for c, (c0, c1) in enumerate(chunks):
        gate = jnp.dot(h_bf16, w13_ref[:, c0:c1], preferred_element_type=F32)
        up = jnp.dot(h_bf16, w13_ref[:, D_FF + c0:D_FF + c1], preferred_element_type=F32)
        if zero is not None:
            gate = gate + zero
        zero = side_work[c]() if side_work is not None else None
        act = (gate * jax.nn.sigmoid(gate) * up).astype(BF16)
        part = jnp.dot(act, w2_ref[c0:c1, :], preferred_element_type=F32)
        acc = part if acc is None else acc + part
    return acc


def _zero_after(v):
    bits = pltpu.bitcast(v[0:V7X_SUBLANES, 0:V7X_LANES], jnp.uint32)
    bits = lax.shift_right_logical(lax.shift_right_logical(bits, jnp.uint32(16)), jnp.uint32(16))
    return pltpu.bitcast(bits, F32)[0:1, 0:1]


def _conv_rows(u_ref, s_ref, cw_ref, base):
    first = HALO - (CONV_WIDTH - 1)
    groups = CONV_ROWS // V7X_SUBLANES
    acc = None
    for kk in range(CONV_WIDTH):
        aligned, r = divmod(first + kk, V7X_SUBLANES)
        src = u_ref if r == 0 else s_ref.at[r - 1]
        lo = base + aligned * V7X_SUBLANES
        term = src[lo:lo + CONV_ROWS, :].reshape(groups, V7X_SUBLANES, D_CONV) * cw_ref[kk]
        acc = term if acc is None else acc + term
    return acc.reshape(CONV_ROWS, D_CONV)


def _load_as_bf16(jobs, stage_ref, sem):
    n_slots = stage_ref.shape[0]
    assert stage_ref.shape[1] >= max(LOAD_ROWS, LOAD_COLS) and stage_ref.shape[2] == LOAD_COLS
    assert all(rows % LOAD_ROWS == 0 and cols % LOAD_COLS == 0 for _, _, rows, cols, _ in jobs)
    tiles = [(src, dst, r0, c0, tr) for src, dst, rows, cols, tr in jobs
             for r0 in range(0, rows, LOAD_ROWS) for c0 in range(0, cols, LOAD_COLS)]

    def stage(t):
        transposed = tiles[t][4]
        shape = (LOAD_COLS, LOAD_ROWS) if transposed else (LOAD_ROWS, LOAD_COLS)
        return stage_ref.at[t % n_slots, 0:shape[0], 0:shape[1]]

    def copy(t):
        src, _, r0, c0, transposed = tiles[t]
        window = (src.at[pl.ds(c0, LOAD_COLS), pl.ds(r0, LOAD_ROWS)] if transposed
                  else src.at[pl.ds(r0, LOAD_ROWS), pl.ds(c0, LOAD_COLS)])
        return pltpu.make_async_copy(window, stage(t), sem.at[t % n_slots])

    for t in range(min(n_slots, len(tiles))):
        copy(t).start()
    for t, (_, dst, r0, c0, transposed) in enumerate(tiles):
        copy(t).wait()
        tile = stage(t)[...]
        dst[r0:r0 + LOAD_ROWS, c0:c0 + LOAD_COLS] = (tile.T if transposed else tile).astype(BF16)
        if t + n_slots < len(tiles):
            copy(t + n_slots).start()


def _ffn_in_body(x_ref, n1_ref, w13_hbm, w2_hbm, nm_ref, wint_hbm, wf_ref, fb_ref, place_ref,
                 cq_ref, ck_ref, cw_ref, cb_ref, lg_ref, lb_ref, on_ref,
                 x1_ref, yc_ref, qt_ref, k_ref, vt_ref, eqt_ref, ek_ref,
                 carry_ref, u_ref, s_ref, cacc_ref, w13_ref, w2_ref, win_ref, load_sem,
                 *, chunks, tm, tiles_per_seq):
    n = pl.program_id(0)

    @pl.when(n == 0)
    def _():
        _load_as_bf16([(w13_hbm, w13_ref, D_MODEL, 2 * D_FF, False),
                       (w2_hbm, w2_ref, D_FF, D_MODEL, False),
                       (wint_hbm, win_ref, D_MODEL, N_IN_MAIN, True)], s_ref, load_sem)
        win_ref[:, N_IN_MAIN:] = wf_ref[...].T.astype(BF16)
        u_ref[...] = jnp.zeros_like(u_ref)
        carry_ref[...] = jnp.zeros_like(carry_ref)

    def tile_sum(v):
        v = v.reshape(-1, V7X_SUBLANES, v.shape[-1]).sum(axis=0)
        return sum(v[:, g:g + V7X_LANES] for g in range(0, v.shape[-1], V7X_LANES))

    def shift_copies():
        for r in range(1, V7X_SUBLANES):
            s_ref[r - 1] = u_ref[r:r + s_ref.shape[1], :]
        return None

    def conv_group(bases):
        def run():
            seen = None
            for base in bases:
                rows = _conv_rows(u_ref, s_ref, cw_ref, base)
                cacc_ref[base:base + CONV_ROWS, :] = rows
                seen = tile_sum(rows) if seen is None else seen + tile_sum(rows)
            return _zero_after(seen)
        return run

    def conv_epilogue():
        cv = cacc_ref[...] + cb_ref[...]
        mu = jnp.mean(cv, axis=-1, keepdims=True)
        xc = cv - mu
        y = xc * lax.rsqrt(jnp.mean(xc * xc, axis=-1, keepdims=True) + EPS) * lg_ref[...] + lb_ref[...]
        y = y * jax.nn.sigmoid(y)
        y = _rms(y, on_ref[...])
        yc_ref[...] = y.astype(BF16)
        return _zero_after(tile_sum(y))

    bases = list(range(0, tm, CONV_ROWS))
    n_groups = len(chunks) - 2
    side_work = ([shift_copies]
                 + [conv_group(bases[g * len(bases) // n_groups:(g + 1) * len(bases) // n_groups])
                    for g in range(n_groups)]
                 + [conv_epilogue])

    flush_step = pl.num_programs(0) - 1

    @pl.when(n == flush_step)
    def _():
        for work in side_work:
            work()

    @pl.when(n < flush_step)
    def _():
        x = x_ref[...]
        h = _rms(x, n1_ref[...]).astype(BF16)
        x1 = x + 0.5 * _swiglu(h, w13_ref, w2_ref, chunks, side_work)
        x1_ref[...] = x1

        h2 = _rms(x1, nm_ref[...]).astype(BF16)
        proj = jnp.dot(h2, win_ref[...], preferred_element_type=F32)

        opens_sequence = lax.rem(n, tiles_per_seq) == 0
        u_ref[0:HALO, :] = jnp.where(opens_sequence, 0.0, u_ref[tm:tm + HALO, :])
        u_ref[HALO:HALO + tm, :] = proj[:, :D_CONV] * jax.nn.sigmoid(proj[:, D_CONV:2 * D_CONV])
        o = 2 * D_CONV
        qt_ref[...] = (proj[:, o:o + D_ATTN] * (HEAD_DIM ** -0.5 * LOG2E)).T.astype(BF16)
        for g in range(N_PAIRS):
            k_ref[g] = proj[:, o + D_ATTN + g * PAIR:o + D_ATTN + (g + 1) * PAIR].astype(BF16)
        vt_ref[...] = proj[:, o + 2 * D_ATTN:o + 3 * D_ATTN].T.astype(BF16)

        z = proj[:, N_IN_MAIN:] + fb_ref[...]
        logf = jnp.minimum(z, 0.0) - jnp.log1p(jnp.exp(-jnp.abs(z)))
        lane = lax.broadcasted_iota(jnp.int32, logf.shape, 1)
        logf = jnp.where(lane < N_HEADS, logf, 0.0)

        row = lax.broadcasted_iota(jnp.int32, (tm, tm), 0)
        col = lax.broadcasted_iota(jnp.int32, (tm, tm), 1)
        tril = jnp.where(row >= col, 1.0, 0.0).astype(BF16)
        d_packed = jnp.dot(tril, _pack3(logf), preferred_element_type=F32)
        d_loc = (d_packed + pltpu.roll(d_packed, F_PAD - N_HEADS, axis=1)
                 + pltpu.roll(d_packed, F_PAD - 2 * N_HEADS, axis=1))
        d_loc = jnp.where(lane < N_HEADS, d_loc, 0.0)

        d = d_loc + jnp.where(opens_sequence, 0.0, carry_ref[0:1, :])
        carry_ref[0:1, :] = d[tm - 1:tm, :]

        placed = jnp.dot(_pack3(d * LOG2E), place_ref[...], preferred_element_type=F32)
        eqt_ref[...] = (placed[:, :D_ATTN] + cq_ref[...]).T.astype(BF16)
        ekn = (placed[:, D_ATTN:] + ck_ref[...]).astype(BF16)
        for g in range(N_PAIRS):
            ek_ref[g] = ekn[:, g * PAIR:(g + 1) * PAIR]


def _attn_body(qt_ref, eqt_ref, k_ref, ek_ref, vt_ref, wa_ref, wb_ref, wc_ref,
               o_ref, wa_out, wb_out, wc_out, ka_ref, va_ref, *, seq, tq):
    for w_in_ref, w_out_ref in ((wa_ref, wa_out), (wb_ref, wb_out), (wc_ref, wc_out)):
        w_out_ref[...] = w_in_ref[...].astype(BF16)

    ka_ref[:, 0:PAIR] = k_ref[...]
    ka_ref[:, PAIR:2 * PAIR] = ek_ref[...]
    for hh in range(2):
        va_ref[hh, 0:HEAD_DIM, :] = vt_ref[hh * HEAD_DIM:(hh + 1) * HEAD_DIM, :]
        va_ref[hh, HEAD_DIM:HEAD_DIM + ONES_ROWS, :] = jnp.ones((ONES_ROWS, seq), BF16)

    key = lax.broadcasted_iota(jnp.int32, (tq, 2 * tq), 0)
    qry = lax.broadcasted_iota(jnp.int32, (tq, 2 * tq), 1)
    causal = key <= jnp.where(qry < tq, qry, qry - tq)
    zeros = jnp.zeros((HEAD_DIM, tq), BF16)

    def masked_queries(i):
        q, e = qt_ref[:, i * tq:(i + 1) * tq], eqt_ref[:, i * tq:(i + 1) * tq]
        h0 = jnp.concatenate([q[0:HEAD_DIM], zeros, e[0:HEAD_DIM], zeros], axis=0)
        h1 = jnp.concatenate([zeros, q[HEAD_DIM:PAIR], zeros, e[HEAD_DIM:PAIR]], axis=0)
        return jnp.concatenate([h0, h1], axis=1)

    qm = {}

    def scores(i, j, w):
        if i not in qm:
            qm.clear()
            qm[i] = masked_queries(i)
        s = jnp.dot(ka_ref[j * tq:(j + w) * tq, :], qm[i], preferred_element_type=F32)
        if j + w <= i:
            return s
        diag = jnp.where(causal, s[(w - 1) * tq:, :], -jnp.inf)
        return diag if w == 1 else jnp.concatenate([s[0:(w - 1) * tq, :], diag], axis=0)

    tasks = [(i, j, min(KEY_BLOCKS, i + 1 - j))
             for i in range(seq // tq) for j in range(0, i + 1, KEY_BLOCKS)]
    s_next = scores(*tasks[0])
    m = acc = None
    for n, (i, j, w) in enumerate(tasks):
        s = s_next
        if n + 1 < len(tasks):
            s_next = scores(*tasks[n + 1])
        m_blk = jnp.max(s, axis=0, keepdims=True)
        m_new = m_blk if j == 0 else jnp.maximum(m, m_blk)
        p = jnp.exp2(s - m_new).astype(BF16)
        o = [jnp.dot(va_ref[hh, :, j * tq:(j + w) * tq], p[:, hh * tq:(hh + 1) * tq],
                     preferred_element_type=F32) for hh in range(2)]
        if j == 0:
            acc = o
        else:
            alpha = jnp.exp2(m - m_new)
            acc = [acc[hh] * alpha[:, hh * tq:(hh + 1) * tq] + o[hh] for hh in range(2)]
        m = m_new
        if j + w == i + 1:
            heads = [a[0:HEAD_DIM, :] * (1.0 / a[HEAD_DIM:HEAD_DIM + 1, :]) for a in acc]
            o_ref[i * tq:(i + 1) * tq, :] = jnp.concatenate(heads, axis=0).T


def _out_ffn_body(x1_ref, yc_ref, ya_ref, na_ref, wo_ref, n2_ref, w13_ref, w2_ref, nf_ref,
                  o_ref, *, chunks):
    ya = jnp.concatenate([ya_ref[g] for g in range(N_PAIRS)], axis=1)
    ya = _rms(ya, na_ref[...]).astype(BF16)
    y = jnp.dot(jnp.concatenate([yc_ref[...], ya], axis=1), wo_ref[...], preferred_element_type=F32)
    x2 = x1_ref[...] + y
    h = _rms(x2, n2_ref[...]).astype(BF16)
    x3 = x2 + 0.5 * _swiglu(h, w13_ref, w2_ref, chunks)
    o_ref[...] = _rms(x3, nf_ref[...])


def _resident(shape):
    nd = len(shape)
    return pl.BlockSpec(shape, lambda *_: (0,) * nd, pipeline_mode=pl.Buffered(1))


def _placement():
    place = np.zeros((F_PAD, 2 * D_ATTN), np.float32)
    cq = np.zeros((1, D_ATTN), np.float32)
    ck = np.zeros((1, D_ATTN), np.float32)
    for hd in range(N_HEADS):
        base = hd * HEAD_DIM
        for i in range(3):
            place[i * N_HEADS + hd, base + i] = 1.0
            place[i * N_HEADS + hd, D_ATTN + base + 3 + i] = -1.0
            cq[0, base + 3 + i] = 1.0
            ck[0, base + i] = 1.0
    return jnp.asarray(place, BF16), jnp.asarray(cq), jnp.asarray(ck)


def _layer(x, p, final_norm, plan):
    bsz, seq, _ = x.shape
    tm, tq = plan["tm"], plan["tq"]
    chunks = plan["ffn_chunks"]
    cparams = functools.partial(pltpu.CompilerParams, vmem_limit_bytes=plan["vmem_limit"])
    row2 = lambda v: v.reshape(1, -1)

    w_in_t = jnp.swapaxes(p["w_in"], 0, 1)
    w_f_t = jnp.pad(w_in_t[N_IN_MAIN:], ((0, F_PAD - N_HEADS), (0, 0)))
    in_hbm = pl.BlockSpec(memory_space=pl.ANY)
    fb = jnp.pad(p["forget_b"], (0, F_PAD - N_HEADS)).reshape(1, F_PAD)
    place, cq, ck = _placement()

    tiles_per_seq = seq // tm
    n_tiles = bsz * tiles_per_seq
    cur = lambda n: jnp.minimum(n, n_tiles - 1)
    flat = lambda width: pl.BlockSpec((tm, width), lambda n: (cur(n), 0))
    flat_t = pl.BlockSpec((None, D_ATTN, tm),
                          lambda n: (cur(n) // tiles_per_seq, 0, cur(n) % tiles_per_seq))
    lagged = pl.BlockSpec((tm, D_CONV), lambda n: (jnp.maximum(n - 1, 0), 0))
    attn_t = jax.ShapeDtypeStruct((bsz, D_ATTN, seq), BF16)
    attn_n = jax.ShapeDtypeStruct((bsz, N_PAIRS, seq, PAIR), BF16)
    flat_p = pl.BlockSpec((None, N_PAIRS, tm, PAIR),
                          lambda n: (cur(n) // tiles_per_seq, 0, cur(n) % tiles_per_seq, 0))
    x1, yc, qt, k, vt, eqt, ek = pl.pallas_call(
        functools.partial(_ffn_in_body, chunks=plan["ffn_in_chunks"], tm=tm,
                          tiles_per_seq=tiles_per_seq),
        grid=(n_tiles + 1,),
        in_specs=[flat(D_MODEL), _resident((1, D_MODEL)), in_hbm, in_hbm, _resident((1, D_MODEL)),
                  in_hbm, _resident((F_PAD, D_MODEL)),
                  _resident((1, F_PAD)), _resident((F_PAD, 2 * D_ATTN)),
                  _resident((1, D_ATTN)), _resident((1, D_ATTN)),
                  _resident((CONV_WIDTH, V7X_SUBLANES, D_CONV))] + [_resident((1, D_CONV))] * 4,
        out_specs=[flat(D_MODEL), lagged, flat_t, flat_p, flat_t, flat_t, flat_p],
        out_shape=[jax.ShapeDtypeStruct((bsz * seq, D_MODEL), F32),
                   jax.ShapeDtypeStruct((bsz * seq, D_CONV), BF16),
                   attn_t, attn_n, attn_t, attn_t, attn_n],
        scratch_shapes=[pltpu.VMEM((V7X_SUBLANES, F_PAD), F32),
                        pltpu.VMEM((HALO + tm, D_CONV), F32),
                        pltpu.VMEM((V7X_SUBLANES - 1, HALO + tm - V7X_SUBLANES, D_CONV), F32),
                        pltpu.VMEM((tm, D_CONV), F32),
                        pltpu.VMEM((D_MODEL, 2 * D_FF), BF16),
                        pltpu.VMEM((D_FF, D_MODEL), BF16),
                        pltpu.VMEM((D_MODEL, N_IN_MAIN + F_PAD), BF16),
                        pltpu.SemaphoreType.DMA((V7X_SUBLANES - 1,))],
        compiler_params=cparams(dimension_semantics=("arbitrary",)),
        name="ffn_in",
    )(x.reshape(bsz * seq, D_MODEL), row2(p["ffn1_norm"]), p["ffn1_w13"], p["ffn1_w2"],
      row2(p["mix_norm"]), w_in_t, w_f_t, fb, place, cq, ck,
      jnp.broadcast_to(p["conv_w"][:, None, :], (CONV_WIDTH, V7X_SUBLANES, D_CONV)),
      row2(p["conv_b"]), row2(p["conv_ln_g"]), row2(p["conv_ln_b"]), row2(p["out_norm_conv"]))
    x1 = x1.reshape(bsz, seq, D_MODEL)
    yc = yc.reshape(bsz, seq, D_CONV)

    pair = pl.BlockSpec((None, None, seq, PAIR), lambda b, g: (b, g, 0, 0))
    pair_t = pl.BlockSpec((None, PAIR, seq), lambda b, g: (b, g, 0))
    steps = bsz * N_PAIRS

    def n_slabs(w):
        return max(d for d in range(1, steps + 1)
                   if steps % d == 0 and w.shape[0] % (d * BF16_SUBLANES) == 0)

    slab = lambda w: w.reshape(n_slabs(w), w.shape[0] // n_slabs(w), w.shape[1])
    slab_spec = lambda w: pl.BlockSpec(
        (None,) + slab(w).shape[1:],
        lambda b, g, per=steps // n_slabs(w): ((b * N_PAIRS + g) // per, 0, 0))
    late_w = [p["w_out"], p["ffn2_w13"], p["ffn2_w2"]]
    ya, w_out, w13_2, w2_2 = pl.pallas_call(
        functools.partial(_attn_body, seq=seq, tq=tq),
        grid=(bsz, N_PAIRS),
        in_specs=[pair_t, pair_t, pair, pair, pair_t] + [slab_spec(w) for w in late_w],
        out_specs=[pair] + [slab_spec(w) for w in late_w],
        out_shape=[jax.ShapeDtypeStruct((bsz, N_PAIRS, seq, PAIR), F32)]
                  + [jax.ShapeDtypeStruct(slab(w).shape, BF16) for w in late_w],
        scratch_shapes=[pltpu.VMEM((seq, 2 * PAIR), BF16),
                        pltpu.VMEM((2, HEAD_DIM + ONES_ROWS, seq), BF16)],
        compiler_params=cparams(dimension_semantics=("arbitrary", "arbitrary")),
        name="attn",
    )(qt, eqt, k, ek, vt, *[slab(w) for w in late_w])
    w_out, w13_2, w2_2 = [w.reshape(src.shape) for w, src in zip((w_out, w13_2, w2_2), late_w)]

    tmo = plan["tm_out"]
    tok = lambda width: pl.BlockSpec((None, tmo, width), lambda b, s: (b, s, 0))
    out = pl.pallas_call(
        functools.partial(_out_ffn_body, chunks=chunks),
        grid=(bsz, seq // tmo),
        in_specs=[tok(D_MODEL), tok(D_CONV),
                  pl.BlockSpec((None, N_PAIRS, tmo, PAIR), lambda b, s: (b, 0, s, 0)),
                  _resident((1, D_ATTN)),
                  _resident((D_MIX, D_MODEL)), _resident((1, D_MODEL)),
                  _resident((D_MODEL, 2 * D_FF)), _resident((D_FF, D_MODEL)),
                  _resident((1, D_MODEL))],
        out_specs=tok(D_MODEL),
        out_shape=jax.ShapeDtypeStruct((bsz, seq, D_MODEL), F32),
        compiler_params=cparams(dimension_semantics=("arbitrary", "arbitrary")),
        name="out_ffn",
    )(x1, yc, ya, row2(p["out_norm_attn"]), w_out, row2(p["ffn2_norm"]), w13_2, w2_2,
      row2(final_norm))
    return out


def kernel(x, ffn1_norm, ffn1_w13, ffn1_w2, mix_norm, w_in, conv_w, conv_b, conv_ln_g, conv_ln_b,
           forget_b, out_norm_conv, out_norm_attn, w_out, ffn2_norm, ffn2_w13, ffn2_w2, final_norm):
    depth = ffn1_norm.shape[0]
    assert depth == 1, "the fused final RMSNorm assumes a single layer"
    plan = _plan()
    stacked = dict(ffn1_norm=ffn1_norm, ffn1_w13=ffn1_w13, ffn1_w2=ffn1_w2, mix_norm=mix_norm,
                   w_in=w_in, conv_w=conv_w, conv_b=conv_b, conv_ln_g=conv_ln_g,
                   conv_ln_b=conv_ln_b, forget_b=forget_b, out_norm_conv=out_norm_conv,
                   out_norm_attn=out_norm_attn, w_out=w_out, ffn2_norm=ffn2_norm,
                   ffn2_w13=ffn2_w13, ffn2_w2=ffn2_w2)
    layer = {name: arr[0] for name, arr in stacked.items()}
    return _layer(x, layer, final_norm, plan)
```

```python
import functools

import numpy as np
import jax
import jax.numpy as jnp
from jax import lax
from jax.experimental import pallas as pl
from jax.experimental.pallas import tpu as pltpu

D_MODEL = 1024
D_CONV = 512
CONV_WIDTH = 31
N_HEADS = 8
HEAD_DIM = 64
D_ATTN = N_HEADS * HEAD_DIM
D_MIX = D_CONV + D_ATTN
D_FF = 2816
EPS = 1e-6
LOG2E = 1.4426950408889634

V7X_LANES = 128
V7X_SUBLANES = 8
BF16_SUBLANES = 16
V7X_MXU_DIM = 256
V7X_VMEM_BYTES = 64 * 1024 * 1024

PAIR = 2 * HEAD_DIM
N_PAIRS = N_HEADS // 2
F_PAD = V7X_LANES
N_IN_MAIN = 2 * D_CONV + 3 * D_ATTN
HALO = 32
CONV_ROWS = 32
LOAD_ROWS, LOAD_COLS = 256, 512
KEY_BLOCKS = 2
ONES_ROWS = 16

BF16 = jnp.bfloat16
F32 = jnp.float32


def _plan():
    def hidden_chunks(mxu_tiles):
        step = mxu_tiles * V7X_MXU_DIM
        return tuple((c, min(c + step, D_FF)) for c in range(0, D_FF, step))

    return dict(
        tm=512,
        tm_out=1024,
        tq=256,
        ffn_chunks=hidden_chunks(6),
        ffn_in_chunks=hidden_chunks(1),
        vmem_limit=V7X_VMEM_BYTES - 8 * 1024 * 1024,
    )


def _rms(x, g):
    return x * lax.rsqrt(jnp.mean(x * x, axis=-1, keepdims=True) + EPS) * g


def _split3(x):
    hi = x.astype(BF16)
    r = x - hi.astype(F32)
    mid = r.astype(BF16)
    lo = (r - mid.astype(F32)).astype(BF16)
    return hi, mid, lo


def _pack3(x):
    hi, mid, lo = _split3(x)
    packed = (hi.astype(F32) + pltpu.roll(mid.astype(F32), N_HEADS, axis=1)
              + pltpu.roll(lo.astype(F32), 2 * N_HEADS, axis=1))
    return packed.astype(BF16)


def _swiglu(h_bf16, w13_ref, w2_ref, chunks, side_work=None):
    acc = None
    zero = None
    for c, (c0, c1) in enumerate(chunks):
        gate = jnp.dot(h_bf16, w13_ref[:, c0:c1], preferred_element_type=F32)
        up = jnp.dot(h_bf16, w13_ref[:, D_FF + c0:D_FF + c1], preferred_element_type=F32)
        if zero is not None:
            gate = gate + zero
        zero = side_work[c]() if side_work is not None else None
        act = (gate * jax.nn.sigmoid(gate) * up).astype(BF16)
        part = jnp.dot(act, w2_ref[c0:c1, :], preferred_element_type=F32)
        acc = part if acc is None else acc + part
    return acc


def _zero_after(v):
    bits = pltpu.bitcast(v[0:V7X_SUBLANES, 0:V7X_LANES], jnp.uint32)
    bits = lax.shift_right_logical(lax.shift_right_logical(bits, jnp.uint32(16)), jnp.uint32(16))
    return pltpu.bitcast(bits, F32)[0:1, 0:1]


def _conv_rows(u_ref, s_ref, cw_ref, base):
    first = HALO - (CONV_WIDTH - 1)
    groups = CONV_ROWS // V7X_SUBLANES
    acc = None
    for kk in range(CONV_WIDTH):
        aligned, r = divmod(first + kk, V7X_SUBLANES)
        src = u_ref if r == 0 else s_ref.at[r - 1]
        lo = base + aligned * V7X_SUBLANES
        term = src[lo:lo + CONV_ROWS, :].reshape(groups, V7X_SUBLANES, D_CONV) * cw_ref[kk]
        acc = term if acc is None else acc + term
    return acc.reshape(CONV_ROWS, D_CONV)


def _load_as_bf16(jobs, stage_ref, sem):
    n_slots = stage_ref.shape[0]
    assert stage_ref.shape[1] >= max(LOAD_ROWS, LOAD_COLS) and stage_ref.shape[2] == LOAD_COLS
    assert all(rows % LOAD_ROWS == 0 and cols % LOAD_COLS == 0 for _, _, rows, cols, _ in jobs)
    tiles = [(src, dst, r0, c0, tr) for src, dst, rows, cols, tr in jobs
             for r0 in range(0, rows, LOAD_ROWS) for c0 in range(0, cols, LOAD_COLS)]

    def stage(t):
        transposed = tiles[t][4]
        shape = (LOAD_COLS, LOAD_ROWS) if transposed else (LOAD_ROWS, LOAD_COLS)
        return stage_ref.at[t % n_slots, 0:shape[0], 0:shape[1]]

    def copy(t):
        src, _, r0, c0, transposed = tiles[t]
        window = (src.at[pl.ds(c0, LOAD_COLS), pl.ds(r0, LOAD_ROWS)] if transposed
                  else src.at[pl.ds(r0, LOAD_ROWS), pl.ds(c0, LOAD_COLS)])
        return pltpu.make_async_copy(window, stage(t), sem.at[t % n_slots])

    for t in range(min(n_slots, len(tiles))):
        copy(t).start()
    for t, (_, dst, r0, c0, transposed) in enumerate(tiles):
        copy(t).wait()
        tile = stage(t)[...]
        dst[r0:r0 + LOAD_ROWS, c0:c0 + LOAD_COLS] = (tile.T if transposed else tile).astype(BF16)
        if t + n_slots < len(tiles):
            copy(t + n_slots).start()


def _ffn_in_body(x_ref, n1_ref, w13_hbm, w2_hbm, nm_ref, wint_hbm, wf_ref, fb_ref, place_ref,
                 cq_ref, ck_ref, cw_ref, cb_ref, lg_ref, lb_ref, on_ref,
                 x1_ref, yc_ref, qt_ref, k_ref, vt_ref, eqt_ref, ek_ref,
                 carry_ref, u_ref, s_ref, cacc_ref, w13_ref, w2_ref, win_ref, load_sem,
                 *, chunks, tm, tiles_per_seq):
    n = pl.program_id(0)

    @pl.when(n == 0)
    def _():
        _load_as_bf16([(w13_hbm, w13_ref, D_MODEL, 2 * D_FF, False),
                       (w2_hbm, w2_ref, D_FF, D_MODEL, False),
                       (wint_hbm, win_ref, D_MODEL, N_IN_MAIN, True)], s_ref, load_sem)
        win_ref[:, N_IN_MAIN:] = wf_ref[...].T.astype(BF16)
        u_ref[...] = jnp.zeros_like(u_ref)
        carry_ref[...] = jnp.zeros_like(carry_ref)

    def tile_sum(v):
        v = v.reshape(-1, V7X_SUBLANES, v.shape[-1]).sum(axis=0)
        return sum(v[:, g:g + V7X_LANES] for g in range(0, v.shape[-1], V7X_LANES))

    def shift_copies():
        for r in range(1, V7X_SUBLANES):
            s_ref[r - 1] = u_ref[r:r + s_ref.shape[1], :]
        return None

    def conv_group(bases):
        def run():
            seen = None
            for base in bases:
                rows = _conv_rows(u_ref, s_ref, cw_ref, base)
                cacc_ref[base:base + CONV_ROWS, :] = rows
                seen = tile_sum(rows) if seen is None else seen + tile_sum(rows)
            return _zero_after(seen)
        return run

    def conv_epilogue():
        cv = cacc_ref[...] + cb_ref[...]
        mu = jnp.mean(cv, axis=-1, keepdims=True)
        xc = cv - mu
        y = xc * lax.rsqrt(jnp.mean(xc * xc, axis=-1, keepdims=True) + EPS) * lg_ref[...] + lb_ref[...]
        y = y * jax.nn.sigmoid(y)
        y = _rms(y, on_ref[...])
        yc_ref[...] = y.astype(BF16)
        return _zero_after(tile_sum(y))

    bases = list(range(0, tm, CONV_ROWS))
    n_groups = len(chunks) - 2
    side_work = ([shift_copies]
                 + [conv_group(bases[g * len(bases) // n_groups:(g + 1) * len(bases) // n_groups])
                    for g in range(n_groups)]
                 + [conv_epilogue])

    flush_step = pl.num_programs(0) - 1

    @pl.when(n == flush_step)
    def _():
        for work in side_work:
            work()

    @pl.when(n < flush_step)
    def _():
        x = x_ref[...]
        h = _rms(x, n1_ref[...]).astype(BF16)
        x1 = x + 0.5 * _swiglu(h, w13_ref, w2_ref, chunks, side_work)
        x1_ref[...] = x1

        h2 = _rms(x1, nm_ref[...]).astype(BF16)
        proj = jnp.dot(h2, win_ref[...], preferred_element_type=F32)

        opens_sequence = lax.rem(n, tiles_per_seq) == 0
        u_ref[0:HALO, :] = jnp.where(opens_sequence, 0.0, u_ref[tm:tm + HALO, :])
        u_ref[HALO:HALO + tm, :] = proj[:, :D_CONV] * jax.nn.sigmoid(proj[:, D_CONV:2 * D_CONV])
        o = 2 * D_CONV
        qt_ref[...] = (proj[:, o:o + D_ATTN] * (HEAD_DIM ** -0.5 * LOG2E)).T.astype(BF16)
        for g in range(N_PAIRS):
            k_ref[g] = proj[:, o + D_ATTN + g * PAIR:o + D_ATTN + (g + 1) * PAIR].astype(BF16)
        vt_ref[...] = proj[:, o + 2 * D_ATTN:o + 3 * D_ATTN].T.astype(BF16)

        z = proj[:, N_IN_MAIN:] + fb_ref[...]
        logf = jnp.minimum(z, 0.0) - jnp.log1p(jnp.exp(-jnp.abs(z)))
        lane = lax.broadcasted_iota(jnp.int32, logf.shape, 1)
        logf = jnp.where(lane < N_HEADS, logf, 0.0)

        row = lax.broadcasted_iota(jnp.int32, (tm, tm), 0)
        col = lax.broadcasted_iota(jnp.int32, (tm, tm), 1)
        tril = jnp.where(row >= col, 1.0, 0.0).astype(BF16)
        d_packed = jnp.dot(tril, _pack3(logf), preferred_element_type=F32)
        d_loc = (d_packed + pltpu.roll(d_packed, F_PAD - N_HEADS, axis=1)
                 + pltpu.roll(d_packed, F_PAD - 2 * N_HEADS, axis=1))
        d_loc = jnp.where(lane < N_HEADS, d_loc, 0.0)

        d = d_loc + jnp.where(opens_sequence, 0.0, carry_ref[0:1, :])
        carry_ref[0:1, :] = d[tm - 1:tm, :]

        placed = jnp.dot(_pack3(d * LOG2E), place_ref[...], preferred_element_type=F32)
        eqt_ref[...] = (placed[:, :D_ATTN] + cq_ref[...]).T.astype(BF16)
        ekn = (placed[:, D_ATTN:] + ck_ref[...]).astype(BF16)
        for g in range(N_PAIRS):
            ek_ref[g] = ekn[:, g * PAIR:(g + 1) * PAIR]


def _attn_body(qt_ref, eqt_ref, k_ref, ek_ref, vt_ref, wa_ref, wb_ref, wc_ref,
               o_ref, wa_out, wb_out, wc_out, ka_ref, va_ref, *, seq, tq):
    for w_in_ref, w_out_ref in ((wa_ref, wa_out), (wb_ref, wb_out), (wc_ref, wc_out)):
        w_out_ref[...] = w_in_ref[...].astype(BF16)

    ka_ref[:, 0:PAIR] = k_ref[...]
    ka_ref[:, PAIR:2 * PAIR] = ek_ref[...]
    for hh in range(2):
        va_ref[hh, 0:HEAD_DIM, :] = vt_ref[hh * HEAD_DIM:(hh + 1) * HEAD_DIM, :]
        va_ref[hh, HEAD_DIM:HEAD_DIM + ONES_ROWS, :] = jnp.ones((ONES_ROWS, seq), BF16)

    key = lax.broadcasted_iota(jnp.int32, (tq, 2 * tq), 0)
    qry = lax.broadcasted_iota(jnp.int32, (tq, 2 * tq), 1)
    causal = key <= jnp.where(qry < tq, qry, qry - tq)
    zeros = jnp.zeros((HEAD_DIM, tq), BF16)

    def masked_queries(i):
        q, e = qt_ref[:, i * tq:(i + 1) * tq], eqt_ref[:, i * tq:(i + 1) * tq]
        h0 = jnp.concatenate([q[0:HEAD_DIM], zeros, e[0:HEAD_DIM], zeros], axis=0)
        h1 = jnp.concatenate([zeros, q[HEAD_DIM:PAIR], zeros, e[HEAD_DIM:PAIR]], axis=0)
        return jnp.concatenate([h0, h1], axis=1)

    qm = {}

    def scores(i, j, w):
        if i not in qm:
            qm.clear()
            qm[i] = masked_queries(i)
        s = jnp.dot(ka_ref[j * tq:(j + w) * tq, :], qm[i], preferred_element_type=F32)
        if j + w <= i:
            return s
        diag = jnp.where(causal, s[(w - 1) * tq:, :], -jnp.inf)
        return diag if w == 1 else jnp.concatenate([s[0:(w - 1) * tq, :], diag], axis=0)

    tasks = [(i, j, min(KEY_BLOCKS, i + 1 - j))
             for i in range(seq // tq) for j in range(0, i + 1, KEY_BLOCKS)]
    s_next = scores(*tasks[0])
    m = acc = None
    for n, (i, j, w) in enumerate(tasks):
        s = s_next
        if n + 1 < len(tasks):
            s_next = scores(*tasks[n + 1])
        m_blk = jnp.max(s, axis=0, keepdims=True)
        m_new = m_blk if j == 0 else jnp.maximum(m, m_blk)
        p = jnp.exp2(s - m_new).astype(BF16)
        o = [jnp.dot(va_ref[hh, :, j * tq:(j + w) * tq], p[:, hh * tq:(hh + 1) * tq],
                     preferred_element_type=F32) for hh in range(2)]
        if j == 0:
            acc = o
        else:
            alpha = jnp.exp2(m - m_new)
            acc = [acc[hh] * alpha[:, hh * tq:(hh + 1) * tq] + o[hh] for hh in range(2)]
        m = m_new
        if j + w == i + 1:
            heads = [a[0:HEAD_DIM, :] * (1.0 / a[HEAD_DIM:HEAD_DIM + 1, :]) for a in acc]
            o_ref[i * tq:(i + 1) * tq, :] = jnp.concatenate(heads, axis=0).T


def _out_ffn_body(x1_ref, yc_ref, ya_ref, na_ref, wo_ref, n2_ref, w13_ref, w2_ref, nf_ref,
                  o_ref, *, chunks):
    ya = jnp.concatenate([ya_ref[g] for g in range(N_PAIRS)], axis=1)
    ya = _rms(ya, na_ref[...]).astype(BF16)
    y = jnp.dot(jnp.concatenate([yc_ref[...], ya], axis=1), wo_ref[...], preferred_element_type=F32)
    x2 = x1_ref[...] + y
    h = _rms(x2, n2_ref[...]).astype(BF16)
    x3 = x2 + 0.5 * _swiglu(h, w13_ref, w2_ref, chunks)
    o_ref[...] = _rms(x3, nf_ref[...])


def _resident(shape):
    nd = len(shape)
    return pl.BlockSpec(shape, lambda *_: (0,) * nd, pipeline_mode=pl.Buffered(1))


def _placement():
    place = np.zeros((F_PAD, 2 * D_ATTN), np.float32)
    cq = np.zeros((1, D_ATTN), np.float32)
    ck = np.zeros((1, D_ATTN), np.float32)
    for hd in range(N_HEADS):
        base = hd * HEAD_DIM
        for i in range(3):
            place[i * N_HEADS + hd, base + i] = 1.0
            place[i * N_HEADS + hd, D_ATTN + base + 3 + i] = -1.0
            cq[0, base + 3 + i] = 1.0
            ck[0, base + i] = 1.0
    return jnp.asarray(place, BF16), jnp.asarray(cq), jnp.asarray(ck)


def _layer(x, p, final_norm, plan):
    bsz, seq, _ = x.shape
    tm, tq = plan["tm"], plan["tq"]
    chunks = plan["ffn_chunks"]
    cparams = functools.partial(pltpu.CompilerParams, vmem_limit_bytes=plan["vmem_limit"])
    row2 = lambda v: v.reshape(1, -1)

    w_in_t = jnp.swapaxes(p["w_in"], 0, 1)
    w_f_t = jnp.pad(w_in_t[N_IN_MAIN:], ((0, F_PAD - N_HEADS), (0, 0)))
    in_hbm = pl.BlockSpec(memory_space=pl.ANY)
    fb = jnp.pad(p["forget_b"], (0, F_PAD - N_HEADS)).reshape(1, F_PAD)
    place, cq, ck = _placement()

    tiles_per_seq = seq // tm
    n_tiles = bsz * tiles_per_seq
    cur = lambda n: jnp.minimum(n, n_tiles - 1)
    flat = lambda width: pl.BlockSpec((tm, width), lambda n: (cur(n), 0))
    flat_t = pl.BlockSpec((None, D_ATTN, tm),
                          lambda n: (cur(n) // tiles_per_seq, 0, cur(n) % tiles_per_seq))
    lagged = pl.BlockSpec((tm, D_CONV), lambda n: (jnp.maximum(n - 1, 0), 0))
    attn_t = jax.ShapeDtypeStruct((bsz, D_ATTN, seq), BF16)
    attn_n = jax.ShapeDtypeStruct((bsz, N_PAIRS, seq, PAIR), BF16)
    flat_p = pl.BlockSpec((None, N_PAIRS, tm, PAIR),
                          lambda n: (cur(n) // tiles_per_seq, 0, cur(n) % tiles_per_seq, 0))
    x1, yc, qt, k, vt, eqt, ek = pl.pallas_call(
        functools.partial(_ffn_in_body, chunks=plan["ffn_in_chunks"], tm=tm,
                          tiles_per_seq=tiles_per_seq),
        grid=(n_tiles + 1,),
        in_specs=[flat(D_MODEL), _resident((1, D_MODEL)), in_hbm, in_hbm, _resident((1, D_MODEL)),
                  in_hbm, _resident((F_PAD, D_MODEL)),
                  _resident((1, F_PAD)), _resident((F_PAD, 2 * D_ATTN)),
                  _resident((1, D_ATTN)), _resident((1, D_ATTN)),
                  _resident((CONV_WIDTH, V7X_SUBLANES, D_CONV))] + [_resident((1, D_CONV))] * 4,
        out_specs=[flat(D_MODEL), lagged, flat_t, flat_p, flat_t, flat_t, flat_p],
        out_shape=[jax.ShapeDtypeStruct((bsz * seq, D_MODEL), F32),
                   jax.ShapeDtypeStruct((bsz * seq, D_CONV), BF16),
                   attn_t, attn_n, attn_t, attn_t, attn_n],
        scratch_shapes=[pltpu.VMEM((V7X_SUBLANES, F_PAD), F32),
                        pltpu.VMEM((HALO + tm, D_CONV), F32),
                        pltpu.VMEM((V7X_SUBLANES - 1, HALO + tm - V7X_SUBLANES, D_CONV), F32),
                        pltpu.VMEM((tm, D_CONV), F32),
                        pltpu.VMEM((D_MODEL, 2 * D_FF), BF16),
                        pltpu.VMEM((D_FF, D_MODEL), BF16),
                        pltpu.VMEM((D_MODEL, N_IN_MAIN + F_PAD), BF16),
                        pltpu.SemaphoreType.DMA((V7X_SUBLANES - 1,))],
        compiler_params=cparams(dimension_semantics=("arbitrary",)),
        name="ffn_in",
    )(x.reshape(bsz * seq, D_MODEL), row2(p["ffn1_norm"]), p["ffn1_w13"], p["ffn1_w2"],
      row2(p["mix_norm"]), w_in_t, w_f_t, fb, place, cq, ck,
      jnp.broadcast_to(p["conv_w"][:, None, :], (CONV_WIDTH, V7X_SUBLANES, D_CONV)),
      row2(p["conv_b"]), row2(p["conv_ln_g"]), row2(p["conv_ln_b"]), row2(p["out_norm_conv"]))
    x1 = x1.reshape(bsz, seq, D_MODEL)
    yc = yc.reshape(bsz, seq, D_CONV)

    pair = pl.BlockSpec((None, None, seq, PAIR), lambda b, g: (b, g, 0, 0))
    pair_t = pl.BlockSpec((None, PAIR, seq), lambda b, g: (b, g, 0))
    steps = bsz * N_PAIRS

    def n_slabs(w):
        return max(d for d in range(1, steps + 1)
                   if steps % d == 0 and w.shape[0] % (d * BF16_SUBLANES) == 0)

    slab = lambda w: w.reshape(n_slabs(w), w.shape[0] // n_slabs(w), w.shape[1])
    slab_spec = lambda w: pl.BlockSpec(
        (None,) + slab(w).shape[1:],
        lambda b, g, per=steps // n_slabs(w): ((b * N_PAIRS + g) // per, 0, 0))
    late_w = [p["w_out"], p["ffn2_w13"], p["ffn2_w2"]]
    ya, w_out, w13_2, w2_2 = pl.pallas_call(
        functools.partial(_attn_body, seq=seq, tq=tq),
        grid=(bsz, N_PAIRS),
        in_specs=[pair_t, pair_t, pair, pair, pair_t] + [slab_spec(w) for w in late_w],
        out_specs=[pair] + [slab_spec(w) for w in late_w],
        out_shape=[jax.ShapeDtypeStruct((bsz, N_PAIRS, seq, PAIR), F32)]
                  + [jax.ShapeDtypeStruct(slab(w).shape, BF16) for w in late_w],
        scratch_shapes=[pltpu.VMEM((seq, 2 * PAIR), BF16),
                        pltpu.VMEM((2, HEAD_DIM + ONES_ROWS, seq), BF16)],
        compiler_params=cparams(dimension_semantics=("arbitrary", "arbitrary")),
        name="attn",
    )(qt, eqt, k, ek, vt, *[slab(w) for w in late_w])
    w_out, w13_2, w2_2 = [w.reshape(src.shape) for w, src in zip((w_out, w13_2, w2_2), late_w)]

    tmo = plan["tm_out"]
    tok = lambda width: pl.BlockSpec((None, tmo, width), lambda b, s: (b, s, 0))
    out = pl.pallas_call(
        functools.partial(_out_ffn_body, chunks=chunks),
        grid=(bsz, seq // tmo),
        in_specs=[tok(D_MODEL), tok(D_CONV),
                  pl.BlockSpec((None, N_PAIRS, tmo, PAIR), lambda b, s: (b, 0, s, 0)),
                  _resident((1, D_ATTN)),
                  _resident((D_MIX, D_MODEL)), _resident((1, D_MODEL)),
                  _resident((D_MODEL, 2 * D_FF)), _resident((D_FF, D_MODEL)),
                  _resident((1, D_MODEL))],
        out_specs=tok(D_MODEL),
        out_shape=jax.ShapeDtypeStruct((bsz, seq, D_MODEL), F32),
        compiler_params=cparams(dimension_semantics=("arbitrary", "arbitrary")),
        name="out_ffn",
    )(x1, yc, ya, row2(p["out_norm_attn"]), w_out, row2(p["ffn2_norm"]), w13_2, w2_2,
      row2(final_norm))
    return out


def kernel(x, ffn1_norm, ffn1_w13, ffn1_w2, mix_norm, w_in, conv_w, conv_b, conv_ln_g, conv_ln_b,
           forget_b, out_norm_conv, out_norm_attn, w_out, ffn2_norm, ffn2_w13, ffn2_w2, final_norm):
    depth = ffn1_norm.shape[0]
    assert depth == 1, "the fused final RMSNorm assumes a single layer"
    plan = _plan()
    stacked = dict(ffn1_norm=ffn1_norm, ffn1_w13=ffn1_w13, ffn1_w2=ffn1_w2, mix_norm=mix_norm,
                   w_in=w_in, conv_w=conv_w, conv_b=conv_b, conv_ln_g=conv_ln_g,
                   conv_ln_b=conv_ln_b, forget_b=forget_b, out_norm_conv=out_norm_conv,
                   out_norm_attn=out_norm_attn, w_out=w_out, ffn2_norm=ffn2_norm,
                   ffn2_w13=ffn2_w13, ffn2_w2=ffn2_w2)
    layer = {name: arr[0] for name, arr in stacked.items()}
    return _layer(x, layer, final_norm, plan)
```
